```python
import numpy as np
import jax
import jax.numpy as jnp
from jax import lax

D_MODEL = 1024
BATCH = 4
SEQ = 4096
DEPTH = 2
DEC_BATCH = 32
DEC_SEQ = 4
PAST_LEN = 16384
PAGE_SIZE = 128

N_A_LAYERS = DEPTH // 2
N_B_LAYERS = DEPTH - N_A_LAYERS
N_DENSE = (DEPTH + 1) // 2
N_MOE = DEPTH // 2

POOL_WINDOWS = (2, 4, 8, 16)
POOL_GROUPS = len(POOL_WINDOWS)
POOL_CH = D_MODEL // POOL_GROUPS
POOL_BUF = max(POOL_WINDOWS) - 1

N_HEADS = 16
HEAD_DIM = D_MODEL // N_HEADS
N_KV_HEADS = 4
HEADS_PER_KV = N_HEADS // N_KV_HEADS
N_BRANCH = 3
CMP_BLOCK = 32
CMP_STRIDE = 16
CMP_HIDDEN = 2 * HEAD_DIM
SEL_BLOCK = 64
N_SEL = 16
WINDOW = 512
Q_BLOCK = 64
FORCE_LOCAL = 2e4
FORCE_INITIAL = 1e4
HIDE_BLOCK = -1e9
NEG_BIG = -1e30

D_FF = 2816
N_EXPERTS = 8
TOP_K = 2
RMS_EPS = 1e-6

N_PAGES = PAST_LEN // PAGE_SIZE
N_PHYS_PAGES = (5 * DEC_BATCH * N_PAGES + 3) // 4
BLOCKS_PER_PAGE = PAGE_SIZE // SEL_BLOCK
WIN_BUF = min(WINDOW, PAST_LEN)
PROMPT_WIN = min(WINDOW, SEQ)

kernel_name = "yoco_pool_nsa_decode_step"


def rmsnorm(x, g):
    xf = x.astype(jnp.float32)
    y = xf * lax.rsqrt(jnp.mean(xf * xf, axis=-1, keepdims=True) + RMS_EPS)
    return (y * g.astype(jnp.float32)).astype(x.dtype)


def masked_softmax(s, mask):
    s = jnp.where(mask, s.astype(jnp.float32), NEG_BIG)
    e = jnp.where(mask, jnp.exp(s - jnp.max(s, axis=-1, keepdims=True)), 0.0)
    return e / jnp.maximum(jnp.sum(e, axis=-1, keepdims=True), 1e-30)


def swiglu(u, w_in, w_out):
    a, b = jnp.split(u @ w_in, 2, axis=-1)
    return (jax.nn.silu(a) * b) @ w_out


def moe_ffn(u, w_router, b_router, w_in, w_out):
    B, T, D = u.shape
    ut = u.reshape(B * T, D)
    logits = (ut @ w_router).astype(jnp.float32) + b_router.astype(jnp.float32)
    top_v, top_i = lax.top_k(logits, TOP_K)
    w = jax.nn.softmax(top_v, axis=-1)
    combine = jnp.einsum('nk,nke->ne', w, (top_i[..., None] == jnp.arange(N_EXPERTS)).astype(jnp.float32)).astype(u.dtype)
    y = jnp.zeros_like(ut)
    for e in range(N_EXPERTS):
        y = y + combine[:, e:e + 1] * swiglu(ut, w_in[e], w_out[e])
    return y.reshape(B, T, D)


def pool_mixer(u, prefix, start, w_pool, scale):
    B, T, D = u.shape
    ext = jnp.concatenate([prefix.astype(u.dtype), u], axis=1)
    cs = jnp.pad(jnp.cumsum(ext.astype(jnp.float32), axis=1), ((0, 0), (1, 0), (0, 0)))
    hi = cs[:, POOL_BUF + 1:]
    pos = start + jnp.arange(T)
    means = []
    for g, w in enumerate(POOL_WINDOWS):
        ch = slice(g * POOL_CH, (g + 1) * POOL_CH)
        lo = cs[:, POOL_BUF + 1 - w:POOL_BUF + 1 - w + T, ch]
        cnt = jnp.minimum(pos + 1, w).astype(jnp.float32)[None, :, None]
        means.append((hi[..., ch] - lo) / cnt)
    pooled = (jnp.concatenate(means, axis=-1) - u.astype(jnp.float32)).astype(u.dtype)
    mixed = jnp.einsum('btgc,gce->btge', pooled.reshape(B, T, POOL_GROUPS, POOL_CH), w_pool).reshape(B, T, D)
    return mixed * scale, ext[:, -POOL_BUF:]


def shared_kv(h, norm_kv, w_kv):
    B, T, _ = h.shape
    return (rmsnorm(h, norm_kv) @ w_kv).reshape(B, T, N_BRANCH, 2, N_KV_HEADS, HEAD_DIM)


def query_side(u, w_qg):
    B, T, _ = u.shape
    qg = u @ w_qg
    q = qg[..., :N_HEADS * HEAD_DIM].reshape(B, T, N_HEADS, HEAD_DIM)
    gates = jax.nn.sigmoid(qg[..., N_HEADS * HEAD_DIM:]).reshape(B, T, N_HEADS, N_BRANCH)
    return q, gates


def compress(kv_tok, w_cmp1, pe_cmp, w_cmp2):
    B, L = kv_tok.shape[:2]
    n_chunks = L // CMP_STRIDE
    c = kv_tok[:, :n_chunks * CMP_STRIDE].reshape(B, n_chunks, CMP_STRIDE, 2, N_KV_HEADS, HEAD_DIM)
    pa = jnp.einsum('bnsagd,asdh->bnagh', c, w_cmp1[:, :CMP_STRIDE])
    pb = jnp.einsum('bnsagd,asdh->bnagh', c, w_cmp1[:, CMP_STRIDE:])
    pe_term = jnp.einsum('asd,asdh->ah', pe_cmp, w_cmp1)
    hid = jax.nn.gelu(pa[:, :-1] + pb[:, 1:] + pe_term[None, None, :, None, :])
    return jnp.einsum('bnagh,ahd->bnagd', hid, w_cmp2)


def selection_map(n_cmp, n_slc):
    i = np.arange(n_cmp)[:, None]
    j = np.arange(n_slc)[None, :]
    ov = (CMP_STRIDE * i < SEL_BLOCK * (j + 1)) & (CMP_STRIDE * i + CMP_BLOCK > SEL_BLOCK * j)
    return jnp.asarray(ov.astype(np.float32))


def gather_kv_blocks(table, lead, blk):
    ex = lambda a: a[..., None, None]
    g = jnp.arange(N_KV_HEADS)[None, :, None, None, None, None]
    rows = table[tuple(ex(a) for a in lead) + (ex(blk), jnp.arange(SEL_BLOCK)[:, None], jnp.arange(2), g)]
    return rows[..., 0, :], rows[..., 1, :]


def nsa_core(q, gates, qpos, kv_cmp, gather_slc, n_slc, k_win, v_win, win_pos):
    B, T = q.shape[:2]
    dt = q.dtype
    qg = q.reshape(B, T, N_KV_HEADS, HEADS_PER_KV, HEAD_DIM) * (HEAD_DIM ** -0.5)
    n_cmp = kv_cmp.shape[1]
    cmp_end = CMP_STRIDE * jnp.arange(n_cmp) + CMP_BLOCK - 1
    s = jnp.einsum('btghd,bngd->bghtn', qg, kv_cmp[:, :, 0])
    p_cmp = masked_softmax(s, (cmp_end[None, :] <= qpos[:, None])[None, None, None])
    o_cmp = jnp.einsum('bghtn,bngd->btghd', p_cmp.astype(dt), kv_cmp[:, :, 1])
    imp = jnp.einsum('bghtn,nj->bgtj', p_cmp, selection_map(n_cmp, n_slc))
    blk = jnp.arange(n_slc)
    visible = blk[None, :] * SEL_BLOCK <= qpos[:, None]
    local = blk[None, :] == (qpos // SEL_BLOCK)[:, None]
    score = jnp.where(local, FORCE_LOCAL, jnp.where(blk == 0, FORCE_INITIAL, jnp.where(visible, imp, HIDE_BLOCK)))
    _, sel = lax.top_k(score, min(N_SEL, n_slc))
    kb, vb = gather_slc(sel)
    n_s = sel.shape[-1]
    kpos = sel[..., None] * SEL_BLOCK + jnp.arange(SEL_BLOCK)
    s = jnp.einsum('btghd,bgtsld->bghtsl', qg, kb)
    smask = (kpos <= qpos[None, None, :, None, None]).reshape(B, N_KV_HEADS, 1, T, n_s * SEL_BLOCK)
    p = masked_softmax(s.reshape(B, N_KV_HEADS, HEADS_PER_KV, T, n_s * SEL_BLOCK), smask)
    o_slc = jnp.einsum('bghtsl,bgtsld->btghd', p.reshape(s.shape).astype(dt), vb)
    s = jnp.einsum('btghd,bkgd->bghtk', qg, k_win)
    wmask = (win_pos[None, :] <= qpos[:, None]) & (win_pos[None, :] > qpos[:, None] - WINDOW) & (win_pos[None, :] >= 0)
    p = masked_softmax(s, wmask[None, None, None])
    o_win = jnp.einsum('bghtk,bkgd->btghd', p.astype(dt), v_win)
    g = gates.reshape(B, T, N_KV_HEADS, HEADS_PER_KV, N_BRANCH)
    o = g[..., 0:1] * o_cmp + g[..., 1:2] * o_slc + g[..., 2:3] * o_win
    return o.reshape(B, T, N_HEADS * HEAD_DIM)


def make_prompt_attend(kv, w_cmp1, pe_cmp, w_cmp2):
    B, T = kv.shape[:2]
    kv_cmp = compress(kv[:, :, 0], w_cmp1, pe_cmp, w_cmp2)
    n_slc = T // SEL_BLOCK
    slc_blocks = kv[:, :, 1].reshape(B, n_slc, SEL_BLOCK, 2, N_KV_HEADS, HEAD_DIM)
    b_idx = jnp.arange(B)[:, None, None, None]
    gather = lambda sel: gather_kv_blocks(slc_blocks, (b_idx,), sel)
    win = jnp.pad(kv[:, :, 2], ((0, 0), (WINDOW, 0), (0, 0), (0, 0), (0, 0)))

    def attend(q, gates):
        def one_block(i):
            q0 = i * Q_BLOCK
            wb = lax.dynamic_slice_in_dim(win, q0, Q_BLOCK + WINDOW, axis=1)
            return nsa_core(lax.dynamic_slice_in_dim(q, q0, Q_BLOCK, axis=1),
                            lax.dynamic_slice_in_dim(gates, q0, Q_BLOCK, axis=1),
                            q0 + jnp.arange(Q_BLOCK), kv_cmp, gather, n_slc,
                            wb[:, :, 0], wb[:, :, 1], q0 - WINDOW + jnp.arange(Q_BLOCK + WINDOW))
        o = lax.map(one_block, jnp.arange(T // Q_BLOCK))
        return o.transpose(1, 0, 2, 3).reshape(B, T, N_HEADS * HEAD_DIM)
    return attend


def make_sample_attend(kv, cache_cmp, cache_slc, cache_win, page_table, w_cmp1, pe_cmp, w_cmp2):
    B, T = kv.shape[:2]
    past_cmp = cache_cmp[page_table].reshape(B, PAST_LEN, 2, N_KV_HEADS, HEAD_DIM).astype(kv.dtype)
    kv_cmp = compress(jnp.concatenate([past_cmp, kv[:, :, 0]], axis=1), w_cmp1, pe_cmp, w_cmp2)
    nb_past = PAST_LEN // SEL_BLOCK
    nb_new = -(-T // SEL_BLOCK)
    pool_blocks = cache_slc.reshape(cache_slc.shape[0] * BLOCKS_PER_PAGE, SEL_BLOCK, 2, N_KV_HEADS, HEAD_DIM)
    new_blocks = jnp.pad(kv[:, :, 1], ((0, 0), (0, nb_new * SEL_BLOCK - T), (0, 0), (0, 0), (0, 0))).reshape(
        B, nb_new, SEL_BLOCK, 2, N_KV_HEADS, HEAD_DIM)
    b_idx = jnp.arange(B)[:, None, None, None]

    def gather(sel):
        jp = jnp.minimum(sel, nb_past - 1)
        phys = page_table[b_idx, jp // BLOCKS_PER_PAGE] * BLOCKS_PER_PAGE + jp % BLOCKS_PER_PAGE
        kp, vp = gather_kv_blocks(pool_blocks, (), phys)
        kn, vn = gather_kv_blocks(new_blocks, (b_idx,), jnp.clip(sel - nb_past, 0, nb_new - 1))
        is_new = (sel >= nb_past)[..., None, None]
        return jnp.where(is_new, kn, kp.astype(kn.dtype)), jnp.where(is_new, vn, vp.astype(vn.dtype))

    win = jnp.concatenate([cache_win.astype(kv.dtype), kv[:, :, 2]], axis=1)
    win_pos = PAST_LEN - WIN_BUF + jnp.arange(WIN_BUF + T)
    qpos = PAST_LEN + jnp.arange(T)

    def attend(q, gates):
        return nsa_core(q, gates, qpos, kv_cmp, gather, nb_past + nb_new, win[:, :, 0], win[:, :, 1], win_pos)
    return attend


def trunk(x, pool_prefix, start, make_attend, p):
    h = x
    pool_states = []
    attend = None
    kv = None
    for l in range(DEPTH):
        u = rmsnorm(h, p['norm_mix'][l])
        if l < N_A_LAYERS:
            mix, st = pool_mixer(u, pool_prefix[l], start, p['w_pool'][l], p['pool_scale'][l])
            pool_states.append(st)
        else:
            lb = l - N_A_LAYERS
            q, gates = query_side(u, p['w_qg'][lb])
            mix = attend(q, gates) @ p['w_o'][lb]
        h = h + mix
        u = rmsnorm(h, p['norm_ffn'][l])
        if l % 2 == 0:
            h = h + swiglu(u, p['w_ffn_in'][l // 2], p['w_ffn_out'][l // 2])
        else:
            h = h + moe_ffn(u, p['w_router'][l // 2], p['b_router'][l // 2], p['w_exp_in'][l // 2], p['w_exp_out'][l // 2])
        if l == N_A_LAYERS - 1:
            kv = shared_kv(h, p['norm_kv'], p['w_kv'])
            attend = make_attend(kv)
    return rmsnorm(h, p['norm_final']), jnp.stack(pool_states), kv


def setup_inputs(seed: int = 0) -> dict:
    key = jax.random.key(seed)
    ks = jax.random.split(key, 32)
    f32 = jnp.float32
    nrm = lambda k, shape, scale: jax.random.normal(k, shape, f32) * scale
    kvw = N_BRANCH * 2 * N_KV_HEADS * HEAD_DIM
    return {
        'x_prompt': nrm(ks[0], (BATCH, SEQ, D_MODEL), 1.0),
        'x_sample': nrm(ks[1], (DEC_BATCH, DEC_SEQ, D_MODEL), 1.0),
        'cache_cmp': nrm(ks[2], (N_PHYS_PAGES, PAGE_SIZE, 2, N_KV_HEADS, HEAD_DIM), 1.0),
        'cache_slc': nrm(ks[3], (N_PHYS_PAGES, PAGE_SIZE, 2, N_KV_HEADS, HEAD_DIM), 1.0),
        'cache_win': nrm(ks[4], (DEC_BATCH, WIN_BUF, 2, N_KV_HEADS, HEAD_DIM), 1.0),
        'state_pool': nrm(ks[5], (N_A_LAYERS, DEC_BATCH, POOL_BUF, D_MODEL), 1.0),
        'page_table': jax.random.permutation(ks[6], N_PHYS_PAGES)[:DEC_BATCH * N_PAGES].reshape(DEC_BATCH, N_PAGES).astype(jnp.int32),
        'norm_mix': 1.0 + nrm(ks[7], (DEPTH, D_MODEL), 0.02),
        'norm_ffn': 1.0 + nrm(ks[8], (DEPTH, D_MODEL), 0.02),
        'norm_kv': 1.0 + nrm(ks[9], (D_MODEL,), 0.02),
        'norm_final': 1.0 + nrm(ks[10], (D_MODEL,), 0.02),
        'w_pool': nrm(ks[11], (N_A_LAYERS, POOL_GROUPS, POOL_CH, POOL_CH), POOL_CH ** -0.5),
        'pool_scale': 1.0 + nrm(ks[12], (N_A_LAYERS, D_MODEL), 0.1),
        'w_kv': nrm(ks[13], (D_MODEL, kvw), D_MODEL ** -0.5),
        'w_cmp1': nrm(ks[14], (2, CMP_BLOCK, HEAD_DIM, CMP_HIDDEN), (CMP_BLOCK * HEAD_DIM) ** -0.5),
        'pe_cmp': nrm(ks[15], (2, CMP_BLOCK, HEAD_DIM), 0.1),
        'w_cmp2': nrm(ks[16], (2, CMP_HIDDEN, HEAD_DIM), CMP_HIDDEN ** -0.5),
        'w_qg': nrm(ks[17], (N_B_LAYERS, D_MODEL, N_HEADS * HEAD_DIM + N_HEADS * N_BRANCH), D_MODEL ** -0.5),
        'w_o': nrm(ks[18], (N_B_LAYERS, N_HEADS * HEAD_DIM, D_MODEL), (N_HEADS * HEAD_DIM) ** -0.5),
        'w_ffn_in': nrm(ks[19], (N_DENSE, D_MODEL, 2 * D_FF), D_MODEL ** -0.5),
        'w_ffn_out': nrm(ks[20], (N_DENSE, D_FF, D_MODEL), D_FF ** -0.5),
        'w_router': nrm(ks[21], (N_MOE, D_MODEL, N_EXPERTS), D_MODEL ** -0.5),
        'b_router': nrm(ks[22], (N_MOE, N_EXPERTS), 0.01),
        'w_exp_in': nrm(ks[23], (N_MOE, N_EXPERTS, D_MODEL, 2 * D_FF), D_MODEL ** -0.5),
        'w_exp_out': nrm(ks[24], (N_MOE, N_EXPERTS, D_FF, D_MODEL), D_FF ** -0.5),
    }


def reference(x_prompt, x_sample, cache_cmp, cache_slc, cache_win, state_pool, page_table,
              norm_mix, norm_ffn, norm_kv, norm_final, w_pool, pool_scale, w_kv, w_cmp1, pe_cmp, w_cmp2,
              w_qg, w_o, w_ffn_in, w_ffn_out, w_router, b_router, w_exp_in, w_exp_out):
    p = dict(norm_mix=norm_mix, norm_ffn=norm_ffn, norm_kv=norm_kv, norm_final=norm_final,
             w_pool=w_pool, pool_scale=pool_scale, w_kv=w_kv, w_qg=w_qg, w_o=w_o,
             w_ffn_in=w_ffn_in, w_ffn_out=w_ffn_out, w_router=w_router, b_router=b_router,
             w_exp_in=w_exp_in, w_exp_out=w_exp_out)
    zero_prefix = jnp.zeros((N_A_LAYERS, x_prompt.shape[0], POOL_BUF, D_MODEL), x_prompt.dtype)
    y_prompt, pool_prompt, kv_p = trunk(
        x_prompt, zero_prefix, 0,
        lambda kv: make_prompt_attend(kv, w_cmp1, pe_cmp, w_cmp2), p)
    y_sample, pool_sample, kv_s = trunk(
        x_sample, state_pool, PAST_LEN,
        lambda kv: make_sample_attend(kv, cache_cmp, cache_slc, cache_win, page_table, w_cmp1, pe_cmp, w_cmp2), p)
    win_sample = jnp.concatenate([cache_win.astype(kv_s.dtype), kv_s[:, :, 2]], axis=1)[:, -WIN_BUF:]
    return (y_prompt, y_sample,
            kv_p[:, :, 0], kv_s[:, :, 0],
            kv_p[:, :, 1], kv_s[:, :, 1],
            kv_p[:, -PROMPT_WIN:, 2], win_sample,
            pool_prompt, pool_sample)
```

```python
import functools

import numpy as np
import jax
import jax.numpy as jnp
from jax import lax
from jax.experimental import pallas as pl
from jax.experimental.pallas import tpu as pltpu

D_MODEL = 1024
POOL_WINDOWS = (2, 4, 8, 16)
POOL_CH = D_MODEL // len(POOL_WINDOWS)
POOL_BUF = max(POOL_WINDOWS) - 1
N_HEADS = 16
HEAD_DIM = 64
N_KV_HEADS = 4
HEADS_PER_KV = 4
N_BRANCH = 3
CMP_BLOCK = 32
CMP_STRIDE = 16
CMP_HIDDEN = 128
SEL_BLOCK = 64
N_SEL = 16
WINDOW = 512
FORCE_LOCAL = 2e4
FORCE_INITIAL = 1e4
HIDE_BLOCK = -1e9
NEG_BIG = -1e30
D_FF = 2816
N_EXPERTS = 8
RMS_EPS = 1e-6
PAGE_SIZE = 128
KV_WIDTH = N_BRANCH * 2 * N_KV_HEADS * HEAD_DIM
BRANCH_WIDTH = 2 * N_KV_HEADS * HEAD_DIM
KV_HALF = N_KV_HEADS * HEAD_DIM
LANES = 128
VMEM_LIMIT = 56 * 1024 * 1024

BF16 = jnp.bfloat16
F32 = jnp.float32


def _params(sem, vmem=VMEM_LIMIT):
    return pltpu.CompilerParams(dimension_semantics=sem, vmem_limit_bytes=vmem)


def _rms(x, g):
    return x * lax.rsqrt(jnp.mean(x * x, axis=-1, keepdims=True) + RMS_EPS) * g


def _dot(a, b):
    return jnp.dot(a, b, preferred_element_type=F32)


def _dot_nt(a, b):
    return lax.dot_general(a, b, (((1,), (1,)), ((), ())), preferred_element_type=F32)


def _split3(x):
    hi = x.astype(BF16)
    r1 = x - hi.astype(F32)
    mid = r1.astype(BF16)
    lo = (r1 - mid.astype(F32)).astype(BF16)
    return hi, mid, lo


def _masked_softmax(s, mask):
    s = jnp.where(mask, s, NEG_BIG)
    e = jnp.where(mask, jnp.exp(s - jnp.max(s, axis=-1, keepdims=True)), 0.0)
    return e / jnp.maximum(jnp.sum(e, axis=-1, keepdims=True), 1e-30)


def _topk_mask(score, k):
    lane = lax.broadcasted_iota(jnp.int32, score.shape, 1)
    width = score.shape[1]
    sel = jnp.zeros(score.shape, F32)
    cur = score
    for _ in range(k):
        m = jnp.max(cur, axis=-1, keepdims=True)
        idx = jnp.min(jnp.where(cur == m, lane, width), axis=-1, keepdims=True)
        hit = lane == idx
        sel = jnp.where(hit, 1.0, sel)
        cur = jnp.where(hit, -jnp.inf, cur)
    return sel


def _pool_prompt_body(x_ref, xp_ref, g_ref, w_ref, sc_ref, o_ref, tail_ref, ext_ref, *, tt):
    i = pl.program_id(1)
    x = x_ref[0]
    g = g_ref[...]
    u = _rms(x, g)
    up = _rms(xp_ref[0], g) * (i > 0).astype(F32)
    ext_ref[0:16] = up
    ext_ref[16:16 + tt] = u
    pos = i * tt + lax.broadcasted_iota(jnp.int32, (tt, 1), 0)
    for gi, w in enumerate(POOL_WINDOWS):
        ch = slice(gi * POOL_CH, (gi + 1) * POOL_CH)
        s = ext_ref[16:16 + tt, ch]
        for k in range(1, w):
            s = s + ext_ref[16 - k:16 - k + tt, ch]
        cnt = jnp.minimum(pos + 1, w).astype(F32)
        pooled = s / cnt - u[:, ch]
        mixed = _dot(pooled.astype(BF16), w_ref[gi])
        o_ref[0, :, ch] = x[:, ch] + mixed * sc_ref[:, ch]
    tail_ref[0] = ext_ref[tt:tt + 16]


def _pool_prompt(x, g, w_pool, scale, tt=512):
    b, t, d = x.shape
    per16 = tt // 16
    return pl.pallas_call(
        functools.partial(_pool_prompt_body, tt=tt),
        grid=(b, t // tt),
        in_specs=[
            pl.BlockSpec((1, tt, d), lambda bi, i: (bi, i, 0)),
            pl.BlockSpec((1, 16, d), lambda bi, i: (bi, jnp.maximum(i * per16 - 1, 0), 0)),
            pl.BlockSpec((1, d), lambda bi, i: (0, 0)),
            pl.BlockSpec((len(POOL_WINDOWS), POOL_CH, POOL_CH), lambda bi, i: (0, 0, 0)),
            pl.BlockSpec((1, d), lambda bi, i: (0, 0)),
        ],
        out_specs=[
            pl.BlockSpec((1, tt, d), lambda bi, i: (bi, i, 0)),
            pl.BlockSpec((1, 16, d), lambda bi, i: (bi, 0, 0)),
        ],
        out_shape=[jax.ShapeDtypeStruct((b, t, d), F32), jax.ShapeDtypeStruct((b, 16, d), F32)],
        scratch_shapes=[pltpu.VMEM((tt + 16, d), F32)],
        compiler_params=_params(("parallel", "arbitrary")),
        name="pool_prompt",
    )(x, x, g, w_pool, scale)


def _pool_sample_body(x_ref, st_ref, g_ref, w_ref, sc_ref, o_ref, nst_ref, ext_ref, *, t):
    u = _rms(x_ref[0], g_ref[...])
    ext_ref[...] = jnp.zeros(ext_ref.shape, F32)
    ext_ref[0:POOL_BUF] = st_ref[0]
    ext_ref[POOL_BUF:POOL_BUF + t] = u
    rows = 16
    for gi, w in enumerate(POOL_WINDOWS):
        ch = slice(gi * POOL_CH, (gi + 1) * POOL_CH)
        s = ext_ref[POOL_BUF:POOL_BUF + rows, ch]
        for k in range(1, w):
            s = s + ext_ref[POOL_BUF - k:POOL_BUF - k + rows, ch]
        pooled = s / float(w) - ext_ref[POOL_BUF:POOL_BUF + rows, ch]
        mixed = _dot(pooled.astype(BF16), w_ref[gi])
        o_ref[0, :, ch] = x_ref[0, :, ch] + mixed[0:t] * sc_ref[:, ch]
    nst_ref[0] = ext_ref[t:t + POOL_BUF]


def _pool_sample(x, state, g, w_pool, scale):
    b, t, d = x.shape
    return pl.pallas_call(
        functools.partial(_pool_sample_body, t=t),
        grid=(b,),
        in_specs=[
            pl.BlockSpec((1, t, d), lambda bi: (bi, 0, 0)),
            pl.BlockSpec((1, POOL_BUF, d), lambda bi: (bi, 0, 0)),
            pl.BlockSpec((1, d), lambda bi: (0, 0)),
            pl.BlockSpec((len(POOL_WINDOWS), POOL_CH, POOL_CH), lambda bi: (0, 0, 0)),
            pl.BlockSpec((1, d), lambda bi: (0, 0)),
        ],
        out_specs=[
            pl.BlockSpec((1, t, d), lambda bi: (bi, 0, 0)),
            pl.BlockSpec((1, POOL_BUF, d), lambda bi: (bi, 0, 0)),
        ],
        out_shape=[jax.ShapeDtypeStruct((b, t, d), F32), jax.ShapeDtypeStruct((b, POOL_BUF, d), F32)],
        scratch_shapes=[pltpu.VMEM((POOL_BUF + 16 + 1, d), F32)],
        compiler_params=_params(("parallel",)),
        name="pool_sample",
    )(x, state, g, w_pool, scale)


def _ffn_body(te_ref, x_ref, g_ref, wa_ref, wb_ref, wo_ref, o_ref, u_ref, acc_ref, *, norm, residual):
    j = pl.program_id(1)

    @pl.when(j == 0)
    def _():
        x = x_ref[...].astype(F32)
        u = _rms(x, g_ref[...]) if norm else x
        u_ref[...] = u.astype(BF16)
        acc_ref[...] = jnp.zeros(acc_ref.shape, F32)

    @pl.when(pl.program_id(0) < te_ref[pl.num_programs(0)])
    def _():
        u = u_ref[...]
        a = _dot(u, wa_ref[0])
        b = _dot(u, wb_ref[0])
        h = (a * jax.nn.sigmoid(a) * b).astype(BF16)
        acc_ref[...] += _dot(h, wo_ref[0])

    @pl.when(j == pl.num_programs(1) - 1)
    def _():
        y = acc_ref[...]
        if residual:
            y = y + x_ref[...].astype(F32)
        o_ref[...] = y


def _ffn(x, g, w_in, w_out, tile_expert, *, norm, residual, tm, tf=1408):
    n, d = x.shape
    nf = D_FF // tf
    grid_spec = pltpu.PrefetchScalarGridSpec(
        num_scalar_prefetch=1,
        grid=(n // tm, nf),
        in_specs=[
            pl.BlockSpec((tm, d), lambda i, j, te: (i, 0)),
            pl.BlockSpec((1, d), lambda i, j, te: (0, 0)),
            pl.BlockSpec((1, d, tf), lambda i, j, te: (te[i], 0, j)),
            pl.BlockSpec((1, d, tf), lambda i, j, te: (te[i], 0, nf + j)),
            pl.BlockSpec((1, tf, d), lambda i, j, te: (te[i], j, 0)),
        ],
        out_specs=pl.BlockSpec((tm, d), lambda i, j, te: (i, 0)),
        scratch_shapes=[pltpu.VMEM((tm, d), BF16), pltpu.VMEM((tm, d), F32)],
    )
    return pl.pallas_call(
        functools.partial(_ffn_body, norm=norm, residual=residual),
        grid_spec=grid_spec,
        out_shape=jax.ShapeDtypeStruct((n, d), F32),
        compiler_params=_params(("parallel", "arbitrary")),
        name="swiglu",
    )(tile_expert, x, g, w_in, w_in, w_out)


def _proj_body(x_ref, gkv_ref, gq_ref, wkv_ref, wq_ref, wg_ref, kv_ref, kvb_ref, q_ref, gate_ref):
    x = x_ref[...]
    ukv = _rms(x, gkv_ref[...]).astype(BF16)
    kv = _dot(ukv, wkv_ref[...])
    kv_ref[...] = kv
    kvb_ref[...] = kv.astype(BF16)
    uq = _rms(x, gq_ref[...]).astype(BF16)
    q_ref[...] = (_dot(uq, wq_ref[...]) * (HEAD_DIM ** -0.5)).astype(BF16)
    gate_ref[...] = jax.nn.sigmoid(_dot(uq, wg_ref[...]))


def _proj(x, g_kv, g_q, w_kv, w_q, w_g, tm):
    n, d = x.shape
    full = lambda shape: pl.BlockSpec(shape, lambda i: (0,) * len(shape))
    return pl.pallas_call(
        _proj_body,
        grid=(n // tm,),
        in_specs=[pl.BlockSpec((tm, d), lambda i: (i, 0)), full((1, d)), full((1, d)),
                  full(w_kv.shape), full(w_q.shape), full(w_g.shape)],
        out_specs=[pl.BlockSpec((tm, KV_WIDTH), lambda i: (i, 0)), pl.BlockSpec((tm, KV_WIDTH), lambda i: (i, 0)),
                   pl.BlockSpec((tm, d), lambda i: (i, 0)), pl.BlockSpec((tm, 2 * LANES), lambda i: (i, 0))],
        out_shape=[jax.ShapeDtypeStruct((n, KV_WIDTH), F32), jax.ShapeDtypeStruct((n, KV_WIDTH), BF16),
                   jax.ShapeDtypeStruct((n, d), BF16), jax.ShapeDtypeStruct((n, 2 * LANES), F32)],
        compiler_params=_params(("parallel",)),
        name="kv_q_proj",
    )(x, g_kv, g_q, w_kv, w_q, w_g)


def _compress_body(*refs, n_src, rows_per_src):
    src = refs[1:1 + n_src]
    w1_ref, pe_ref, w2_ref, o_ref, carry_ref = refs[1 + n_src:]
    j = pl.program_id(2)
    chunks = rows_per_src // CMP_STRIDE
    n = n_src * chunks

    @pl.when(j == 0)
    def _():
        carry_ref[...] = jnp.zeros(carry_ref.shape, F32)

    w1 = w1_ref[0]
    pe_out = _dot(pe_ref[0], w1)
    pe_term = pe_out[0:1, 0:CMP_HIDDEN] + pe_out[1:2, CMP_HIDDEN:2 * CMP_HIDDEN]
    row = lax.broadcasted_iota(jnp.int32, (n, 1), 0)
    lhs = jnp.concatenate(
        [jnp.concatenate([r[0, pl.ds(s, chunks, stride=CMP_STRIDE), :] for r in src], axis=0)
         for s in range(CMP_STRIDE)], axis=1).astype(BF16)
    p = _dot(lhs, w1)
    for g2 in range(2):
        pa = p[:, g2 * 2 * CMP_HIDDEN:g2 * 2 * CMP_HIDDEN + CMP_HIDDEN]
        pb = p[:, g2 * 2 * CMP_HIDDEN + CMP_HIDDEN:(g2 + 1) * 2 * CMP_HIDDEN]
        pa_prev = jnp.where(row == 0, carry_ref[g2:g2 + 1], pltpu.roll(pa, 1, 0))
        carry_ref[g2:g2 + 1] = pa[n - 1:n]
        hid = jax.nn.gelu(pa_prev + pb + pe_term)
        o_ref[0, 0, g2] = _dot(hid.astype(BF16), w2_ref[0])


def _compress(src_arrays, src_specs, grid, table, w1, pe, w2, rows_per_src, n_rows_total):
    n_src = len(src_specs)
    n = n_src * rows_per_src // CMP_STRIDE
    b = grid[0]
    grid_spec = pltpu.PrefetchScalarGridSpec(
        num_scalar_prefetch=1,
        grid=grid,
        in_specs=list(src_specs) + [
            pl.BlockSpec((1,) + w1.shape[1:], lambda bi, c, j, tb: (c // 2, 0, 0)),
            pl.BlockSpec((1,) + pe.shape[1:], lambda bi, c, j, tb: (c // 2, 0, 0)),
            pl.BlockSpec((1,) + w2.shape[1:], lambda bi, c, j, tb: (c // 2, 0, 0)),
        ],
        out_specs=pl.BlockSpec((1, 1, 2, n, HEAD_DIM), lambda bi, c, j, tb: (bi, c // 2, c % 2, j, 0)),
        scratch_shapes=[pltpu.VMEM((8, CMP_HIDDEN), F32)],
    )
    return pl.pallas_call(
        functools.partial(_compress_body, n_src=n_src, rows_per_src=rows_per_src),
        grid_spec=grid_spec,
        out_shape=jax.ShapeDtypeStruct((b, 2, N_KV_HEADS, n_rows_total, HEAD_DIM), F32),
        compiler_params=_params(("parallel", "parallel", "arbitrary")),
        name="compress",
    )(table, *src_arrays, w1, pe, w2)


def _flash_chunks(q_st, k_ref, v_ref, lane0, c_lo, c_hi, mask_fn, rows):
    def body(c, carry):
        m, l, acc = carry
        k = k_ref[0, pl.ds(pl.multiple_of(c * LANES, LANES), LANES), lane0:lane0 + HEAD_DIM]
        v = v_ref[0, pl.ds(pl.multiple_of(c * LANES, LANES), LANES), lane0:lane0 + HEAD_DIM]
        mask = mask_fn(c)
        s = jnp.where(mask, _dot_nt(q_st, k), NEG_BIG)
        m_new = jnp.maximum(m, jnp.max(s, axis=-1, keepdims=True))
        alpha = jnp.exp(m - m_new)
        p = jnp.where(mask, jnp.exp(s - m_new), 0.0)
        l = alpha * l + jnp.sum(p, axis=-1, keepdims=True)
        acc = alpha * acc + _dot(p.astype(BF16), v)
        return m_new, l, acc

    init = (jnp.full((rows, 1), NEG_BIG, F32), jnp.zeros((rows, 1), F32), jnp.zeros((rows, HEAD_DIM), F32))
    m, l, acc = lax.fori_loop(c_lo, c_hi, body, init)
    return acc / l


def _attn_prompt_body(q_ref, gate_ref, kvc_ref, ks_ref, vs_ref, kw_ref, vw_ref, msel_ref, o_ref, *, tq, n_cmp_rows, n_slc):
    i = pl.program_id(2)
    q0 = i * tq
    rows = HEADS_PER_KV * tq
    tok = lax.broadcasted_iota(jnp.int32, (tq, 1), 0)
    qpos_t = q0 + tok
    qpos = jnp.concatenate([qpos_t] * HEADS_PER_KV, axis=0)
    key_lane = lax.broadcasted_iota(jnp.int32, (1, LANES), 1)
    cmp_row = lax.broadcasted_iota(jnp.int32, (1, n_cmp_rows), 1)
    cmp_vis = (cmp_row >= 1) & (CMP_STRIDE * cmp_row + CMP_STRIDE - 1 <= qpos)
    blk = lax.broadcasted_iota(jnp.int32, (1, LANES), 1)
    e_row = lax.broadcasted_iota(jnp.int32, (LANES, LANES), 0)
    e_col = lax.broadcasted_iota(jnp.int32, (LANES, LANES), 1)
    q_all = q_ref[0]
    gates = gate_ref[0]
    outs = []
    for g2 in range(2):
        q_st = jnp.concatenate(
            [q_all[:, (g2 * HEADS_PER_KV + hh) * HEAD_DIM:(g2 * HEADS_PER_KV + hh + 1) * HEAD_DIM]
             for hh in range(HEADS_PER_KV)], axis=0)
        kc = kvc_ref[0, 0, g2].astype(BF16)
        vc = kvc_ref[0, 1, g2].astype(BF16)
        p_cmp = _masked_softmax(_dot_nt(q_st, kc), cmp_vis)
        o_cmp = _dot(p_cmp.astype(BF16), vc)
        p_sum = p_cmp[0:tq]
        for hh in range(1, HEADS_PER_KV):
            p_sum = p_sum + p_cmp[hh * tq:(hh + 1) * tq]
        hi, mid, lo = _split3(p_sum)
        msel = msel_ref[...]
        imp = _dot(hi, msel) + _dot(mid, msel) + _dot(lo, msel)
        local = blk == jnp.right_shift(qpos_t, 6)
        visible = blk * SEL_BLOCK <= qpos_t
        score = jnp.where(local, FORCE_LOCAL, jnp.where(blk == 0, FORCE_INITIAL, jnp.where(visible, imp, HIDE_BLOCK)))
        score = jnp.where(blk < n_slc, score, -jnp.inf)
        sel = _topk_mask(score, min(N_SEL, n_slc)).astype(BF16)

        def slc_mask(c):
            expand = (e_row == 2 * c + (e_col >= SEL_BLOCK).astype(jnp.int32)).astype(BF16)
            chosen = _dot(sel, expand)
            ok = jnp.where(c * LANES + key_lane <= qpos_t, chosen, 0.0)
            return jnp.concatenate([ok] * HEADS_PER_KV, axis=0) > 0.5

        def win_mask(c):
            kpos = c * LANES + key_lane
            ok = ((kpos <= qpos_t) & (kpos > qpos_t - WINDOW)).astype(F32)
            return jnp.concatenate([ok] * HEADS_PER_KV, axis=0) > 0.5

        lane0 = g2 * HEAD_DIM
        c_hi = (q0 + tq + LANES - 1) // LANES
        o_slc = _flash_chunks(q_st, ks_ref, vs_ref, lane0, 0, c_hi, slc_mask, rows)
        c_lo = jnp.maximum(q0 - WINDOW + 1, 0) // LANES
        o_win = _flash_chunks(q_st, kw_ref, vw_ref, lane0, c_lo, c_hi, win_mask, rows)
        for hh in range(HEADS_PER_KV):
            col = (g2 * HEADS_PER_KV + hh) * N_BRANCH
            rs = slice(hh * tq, (hh + 1) * tq)
            outs.append(gates[:, col:col + 1] * o_cmp[rs] + gates[:, col + 1:col + 2] * o_slc[rs]
                        + gates[:, col + 2:col + 3] * o_win[rs])
    o_ref[0] = jnp.concatenate(outs, axis=1)


def _attn_prompt(q, gates, kvc, kvb, msel, tq=128):
    b, t, d = q.shape
    r = kvc.shape[3]
    half = 2 * HEAD_DIM
    kv_col = lambda base: (lambda bi, gp, i: (bi, 0, base + gp))
    kv_spec = lambda base: pl.BlockSpec((1, t, half), kv_col(base))
    slc0 = BRANCH_WIDTH // half
    win0 = 2 * BRANCH_WIDTH // half
    vofs = KV_HALF // half
    return pl.pallas_call(
        functools.partial(_attn_prompt_body, tq=tq, n_cmp_rows=r, n_slc=t // SEL_BLOCK),
        grid=(b, 2, t // tq),
        in_specs=[
            pl.BlockSpec((1, tq, 2 * HEADS_PER_KV * HEAD_DIM), lambda bi, gp, i: (bi, i, gp)),
            pl.BlockSpec((1, tq, LANES), lambda bi, gp, i: (bi, i, gp)),
            pl.BlockSpec((1, 2, 2, r, HEAD_DIM), lambda bi, gp, i: (bi, 0, gp, 0, 0)),
            kv_spec(slc0), kv_spec(slc0 + vofs), kv_spec(win0), kv_spec(win0 + vofs),
            pl.BlockSpec(msel.shape, lambda bi, gp, i: (0, 0)),
        ],
        out_specs=pl.BlockSpec((1, tq, 2 * HEADS_PER_KV * HEAD_DIM), lambda bi, gp, i: (bi, i, gp)),
        out_shape=jax.ShapeDtypeStruct((b, t, d), F32),
        compiler_params=_params(("parallel", "parallel", "arbitrary")),
        name="attn_prompt",
    )(q, gates, kvc, kvb, kvb, kvb, kvb, msel)


T_PAD = 8
S_ROWS = N_KV_HEADS * HEADS_PER_KV * T_PAD


def _attn_sample_cmp_body(q_ref, kvc_ref, msel_ref, o_ref, sel_ref, *, past_len, n_cmp_rows, n_slc):
    tok = lax.broadcasted_iota(jnp.int32, (T_PAD, 1), 0)
    qpos_t = past_len + tok
    qpos = jnp.concatenate([qpos_t] * HEADS_PER_KV, axis=0)
    cmp_row = lax.broadcasted_iota(jnp.int32, (1, n_cmp_rows), 1)
    cmp_vis = (cmp_row >= 1) & (CMP_STRIDE * cmp_row + CMP_STRIDE - 1 <= qpos)
    blk = lax.broadcasted_iota(jnp.int32, (1, msel_ref.shape[1]), 1)
    msel = msel_ref[...]
    for g in range(N_KV_HEADS):
        q_st = q_ref[0, g]
        kc = kvc_ref[0, 0, g].astype(BF16)
        vc = kvc_ref[0, 1, g].astype(BF16)
        p_cmp = _masked_softmax(_dot_nt(q_st, kc), cmp_vis)
        o_ref[0, g] = _dot(p_cmp.astype(BF16), vc)
        p_sum = p_cmp[0:T_PAD]
        for hh in range(1, HEADS_PER_KV):
            p_sum = p_sum + p_cmp[hh * T_PAD:(hh + 1) * T_PAD]
        hi, mid, lo = _split3(p_sum)
        imp = _dot(hi, msel) + _dot(mid, msel) + _dot(lo, msel)
        local = blk == jnp.right_shift(qpos_t, 6)
        visible = blk * SEL_BLOCK <= qpos_t
        score = jnp.where(local, FORCE_LOCAL, jnp.where(blk == 0, FORCE_INITIAL, jnp.where(visible, imp, HIDE_BLOCK)))
        score = jnp.where(blk < n_slc, score, -jnp.inf)
        sel_ref[0, g] = _topk_mask(score, min(N_SEL, n_slc))


def _attn_sample_cmp(q32, kvc, msel, past_len, n_slc):
    b = q32.shape[0]
    r = kvc.shape[3]
    w = msel.shape[1]
    return pl.pallas_call(
        functools.partial(_attn_sample_cmp_body, past_len=past_len, n_cmp_rows=r, n_slc=n_slc),
        grid=(b,),
        in_specs=[
            pl.BlockSpec((1, N_KV_HEADS, HEADS_PER_KV * T_PAD, HEAD_DIM), lambda bi: (bi, 0, 0, 0)),
            pl.BlockSpec((1, 2, N_KV_HEADS, r, HEAD_DIM), lambda bi: (bi, 0, 0, 0, 0)),
            pl.BlockSpec(msel.shape, lambda bi: (0, 0)),
        ],
        out_specs=[
            pl.BlockSpec((1, N_KV_HEADS, HEADS_PER_KV * T_PAD, HEAD_DIM), lambda bi: (bi, 0, 0, 0)),
            pl.BlockSpec((1, N_KV_HEADS, T_PAD, w), lambda bi: (bi, 0, 0, 0)),
        ],
        out_shape=[jax.ShapeDtypeStruct((b, N_KV_HEADS, HEADS_PER_KV * T_PAD, HEAD_DIM), F32),
                   jax.ShapeDtypeStruct((b, N_KV_HEADS, T_PAD, w), F32)],
        compiler_params=_params(("parallel",)),
        name="attn_sample_cmp",
    )(q32, kvc, msel)


def _attn_sample_body(*refs, n_pages, t_new):
    pages = refs[1:1 + n_pages]
    (q_ref, sel_ref, new_ref, win_ref, ocmp_ref, gate_ref, o_ref, m_ref, l_ref, acc_ref) = refs[1 + n_pages:]
    j = pl.program_id(1)
    row = lax.broadcasted_iota(jnp.int32, (S_ROWS, 1), 0)
    tok = jnp.bitwise_and(row, T_PAD - 1)
    row_g = jnp.right_shift(row, 5)
    lane = lax.broadcasted_iota(jnp.int32, (1, LANES), 1)
    q = q_ref[0]

    @pl.when(j == 0)
    def _():
        m_ref[...] = jnp.full(m_ref.shape, NEG_BIG, F32)
        l_ref[...] = jnp.zeros(l_ref.shape, F32)
        acc_ref[...] = jnp.zeros(acc_ref.shape, F32)

    sel = sel_ref[0, 0]
    s_parts, m_parts = [], []
    for pi in range(n_pages):
        k = pages[pi][0, :, 0:KV_HALF].astype(BF16)
        s_parts.append(_dot_nt(q, k))
        chosen = jnp.where(lane < SEL_BLOCK, sel[:, 2 * pi:2 * pi + 1], sel[:, 2 * pi + 1:2 * pi + 2])
        m_parts.append(chosen > 0.5)
    mask = jnp.concatenate(m_parts, axis=1)
    s = jnp.where(mask, jnp.concatenate(s_parts, axis=1), NEG_BIG)
    m_old = m_ref[...]
    m_new = jnp.maximum(m_old, jnp.max(s, axis=-1, keepdims=True))
    alpha = jnp.exp(m_old - m_new)
    p = jnp.where(mask, jnp.exp(s - m_new), 0.0)
    l_ref[...] = alpha * l_ref[...] + jnp.sum(p, axis=-1, keepdims=True)
    pv = jnp.zeros((S_ROWS, KV_HALF), F32)
    for pi in range(n_pages):
        v = pages[pi][0, :, KV_HALF:2 * KV_HALF].astype(BF16)
        pv = pv + _dot(p[:, pi * LANES:(pi + 1) * LANES].astype(BF16), v)
    acc_ref[...] = alpha * acc_ref[...] + pv
    m_ref[...] = m_new

    @pl.when(j == pl.num_programs(1) - 1)
    def _():
        def own_head(x):
            out = jnp.zeros((S_ROWS, HEAD_DIM), F32)
            for g in range(N_KV_HEADS):
                out = out + jnp.where(row_g == g, x[:, g * HEAD_DIM:(g + 1) * HEAD_DIM], 0.0)
            return out

        new_ok = (lane <= tok) & (lane < t_new)
        k_new = new_ref[0, 0, :, 0:KV_HALF].astype(BF16)
        v_new = new_ref[0, 0, :, KV_HALF:2 * KV_HALF].astype(BF16)
        s_n = jnp.where(new_ok, _dot_nt(q, k_new), NEG_BIG)
        m_old = m_ref[...]
        m_fin = jnp.maximum(m_old, jnp.max(s_n, axis=-1, keepdims=True))
        alpha = jnp.exp(m_old - m_fin)
        p_n = jnp.where(new_ok, jnp.exp(s_n - m_fin), 0.0)
        l_fin = alpha * l_ref[...] + jnp.sum(p_n, axis=-1, keepdims=True)
        o_slc = own_head((alpha * acc_ref[...] + _dot(p_n.astype(BF16), v_new)) / l_fin)
        kw = win_ref[0, :, 0:KV_HALF].astype(BF16)
        vw = win_ref[0, :, KV_HALF:2 * KV_HALF].astype(BF16)
        kw_new = new_ref[0, 1, :, 0:KV_HALF].astype(BF16)
        vw_new = new_ref[0, 1, :, KV_HALF:2 * KV_HALF].astype(BF16)
        widx = lax.broadcasted_iota(jnp.int32, (1, win_ref.shape[1]), 1)
        w_mask = jnp.concatenate([jnp.broadcast_to(widx > tok, (S_ROWS, win_ref.shape[1])),
                                  jnp.broadcast_to(new_ok, (S_ROWS, LANES))], axis=1)
        s_w = jnp.concatenate([_dot_nt(q, kw), _dot_nt(q, kw_new)], axis=1)
        p_w = _masked_softmax(s_w, w_mask)
        nw = win_ref.shape[1]
        o_win = own_head(_dot(p_w[:, 0:nw].astype(BF16), vw) + _dot(p_w[:, nw:].astype(BF16), vw_new))
        gates = gate_ref[0]
        o_ref[0] = gates[:, 0:1] * ocmp_ref[0] + gates[:, 1:2] * o_slc + gates[:, 2:3] * o_win


def _attn_sample(page_table, cache_slc, q_st, sel_steps, kv_new, t_new, cache_win, o_cmp, gate_rows, n_pages_step):
    b, n_pages = page_table.shape
    steps = n_pages // n_pages_step
    page_spec = lambda pi: pl.BlockSpec(
        (1, PAGE_SIZE, BRANCH_WIDTH), lambda bi, j, pt: (pt[bi, j * n_pages_step + pi], 0, 0))
    grid_spec = pltpu.PrefetchScalarGridSpec(
        num_scalar_prefetch=1,
        grid=(b, steps),
        in_specs=[page_spec(pi) for pi in range(n_pages_step)] + [
            pl.BlockSpec((1, S_ROWS, KV_HALF), lambda bi, j, pt: (bi, 0, 0)),
            pl.BlockSpec((1, 1, S_ROWS, LANES), lambda bi, j, pt: (bi, j, 0, 0)),
            pl.BlockSpec((1, 2, LANES, BRANCH_WIDTH), lambda bi, j, pt: (bi, 0, 0, 0)),
            pl.BlockSpec((1,) + cache_win.shape[1:], lambda bi, j, pt: (bi, 0, 0)),
            pl.BlockSpec((1, S_ROWS, HEAD_DIM), lambda bi, j, pt: (bi, 0, 0)),
            pl.BlockSpec((1, S_ROWS, LANES), lambda bi, j, pt: (bi, 0, 0)),
        ],
        out_specs=pl.BlockSpec((1, S_ROWS, HEAD_DIM), lambda bi, j, pt: (bi, 0, 0)),
        scratch_shapes=[pltpu.VMEM((S_ROWS, 1), F32), pltpu.VMEM((S_ROWS, 1), F32), pltpu.VMEM((S_ROWS, KV_HALF), F32)],
    )
    return pl.pallas_call(
        functools.partial(_attn_sample_body, n_pages=n_pages_step, t_new=t_new),
        grid_spec=grid_spec,
        out_shape=jax.ShapeDtypeStruct((b, S_ROWS, HEAD_DIM), F32),
        compiler_params=_params(("parallel", "arbitrary")),
        name="attn_sample",
    )(page_table, *([cache_slc] * n_pages_step), q_st, sel_steps, kv_new, cache_win, o_cmp, gate_rows)


def _oproj_router_body(h_ref, a_ref, wo_ref, g_ref, wr_hi_ref, wr_lo_ref, br_ref, h3_ref, u_ref, route_ref):
    h3 = h_ref[...] + _dot(a_ref[...].astype(BF16), wo_ref[...])
    h3_ref[...] = h3
    u = _rms(h3, g_ref[...])
    u_ref[...] = u.astype(BF16)
    u_hi = u.astype(BF16)
    u_lo = (u - u_hi.astype(F32)).astype(BF16)
    logits = _dot(u_hi, wr_hi_ref[...]) + _dot(u_hi, wr_lo_ref[...]) + _dot(u_lo, wr_hi_ref[...]) + br_ref[...]
    lane = lax.broadcasted_iota(jnp.int32, logits.shape, 1)
    logits = jnp.where(lane < N_EXPERTS, logits, -jnp.inf)
    v1 = jnp.max(logits, axis=-1, keepdims=True)
    i1 = jnp.min(jnp.where(logits == v1, lane, LANES), axis=-1, keepdims=True)
    rest = jnp.where(lane == i1, -jnp.inf, logits)
    v2 = jnp.max(rest, axis=-1, keepdims=True)
    i2 = jnp.min(jnp.where(rest == v2, lane, LANES), axis=-1, keepdims=True)
    e2 = jnp.exp(v2 - v1)
    w1 = 1.0 / (1.0 + e2)
    w2 = e2 / (1.0 + e2)
    route_ref[...] = jnp.where(lane == 0, i1.astype(F32), jnp.where(lane == 1, i2.astype(F32),
                               jnp.where(lane == 2, w1, jnp.where(lane == 3, w2, 0.0))))


def _oproj_router(h, attn, w_o, g, wr_hi, wr_lo, b_r, tm):
    n, d = h.shape
    full = lambda shape: pl.BlockSpec(shape, lambda i: (0,) * len(shape))
    tile = lambda w: pl.BlockSpec((tm, w), lambda i: (i, 0))
    return pl.pallas_call(
        _oproj_router_body,
        grid=(n // tm,),
        in_specs=[tile(d), tile(d), full(w_o.shape), full((1, d)), full(wr_hi.shape), full(wr_lo.shape), full(b_r.shape)],
        out_specs=[tile(d), tile(d), tile(LANES)],
        out_shape=[jax.ShapeDtypeStruct((n, d), F32), jax.ShapeDtypeStruct((n, d), BF16),
                   jax.ShapeDtypeStruct((n, LANES), F32)],
        compiler_params=_params(("parallel",)),
        name="oproj_router",
    )(h, attn, w_o, g, wr_hi, wr_lo, b_r)


def _final_body(h_ref, y1_ref, y2_ref, route_ref, g_ref, o_ref):
    r = route_ref[...]
    h4 = h_ref[...] + (r[:, 2:3] * y1_ref[...] + r[:, 3:4] * y2_ref[...])
    o_ref[...] = _rms(h4, g_ref[...])


def _final(h, y1, y2, route, g, tm):
    n, d = h.shape
    tile = lambda w: pl.BlockSpec((tm, w), lambda i: (i, 0))
    return pl.pallas_call(
        _final_body,
        grid=(n // tm,),
        in_specs=[tile(d), tile(d), tile(d), tile(LANES), pl.BlockSpec((1, d), lambda i: (0, 0))],
        out_specs=tile(d),
        out_shape=jax.ShapeDtypeStruct((n, d), F32),
        compiler_params=_params(("parallel",)),
        name="moe_combine_norm",
    )(h, y1, y2, route, g)


def _selection_rows(n_rows, width):
    r = np.arange(n_rows)[:, None]
    j = np.arange(width)[None, :]
    i = r - 1
    ov = (r >= 1) & (CMP_STRIDE * i < SEL_BLOCK * (j + 1)) & (CMP_STRIDE * i + CMP_BLOCK > SEL_BLOCK * j)
    return jnp.asarray(ov.astype(np.float32), dtype=BF16)


def _compress_weights(w_cmp1, pe_cmp):
    w = w_cmp1.reshape(2, 2, CMP_STRIDE, HEAD_DIM, CMP_HIDDEN).transpose(0, 2, 3, 1, 4)
    eye = jnp.eye(2, dtype=w.dtype)
    w1 = jnp.einsum('asdvh,gk->asgdkvh', w, eye).reshape(2, CMP_STRIDE * 2 * HEAD_DIM, 4 * CMP_HIDDEN)
    pe = pe_cmp.reshape(2, 2, CMP_STRIDE, 1, HEAD_DIM)
    pe = jnp.concatenate([pe, jnp.zeros_like(pe)], axis=3).reshape(2, 2, CMP_STRIDE * 2 * HEAD_DIM)
    pe = jnp.concatenate([pe, jnp.zeros((2, 6, pe.shape[-1]), pe.dtype)], axis=1)
    return w1.astype(BF16), pe.astype(BF16)


def _gate_weights(w_qg_l):
    wg = w_qg_l[:, N_HEADS * HEAD_DIM:].reshape(D_MODEL, 2, 2 * HEADS_PER_KV * N_BRANCH)
    wg = jnp.pad(wg, ((0, 0), (0, 0), (0, LANES - wg.shape[-1])))
    return wg.reshape(D_MODEL, 2 * LANES)


def _dispatch(route, tm):
    n = route.shape[0]
    e_idx = route[:, 0:2].astype(jnp.int32)
    flat = e_idx.reshape(-1)
    onehot = (flat[:, None] == jnp.arange(N_EXPERTS)[None, :]).astype(jnp.int32)
    counts = jnp.sum(onehot, axis=0)
    padded = ((counts + tm - 1) // tm) * tm
    starts = jnp.cumsum(padded) - padded
    rank = jnp.take_along_axis(jnp.cumsum(onehot, axis=0), flat[:, None], axis=1)[:, 0] - 1
    dest = starts[flat] + rank
    n_tiles = -(-2 * n // tm) + N_EXPERTS
    src = jnp.zeros((n_tiles * tm,), jnp.int32).at[dest].set(jnp.arange(2 * n, dtype=jnp.int32) // 2)
    tile_start = jnp.arange(n_tiles, dtype=jnp.int32) * tm
    ends = jnp.cumsum(padded)
    tile_expert = jnp.minimum(jnp.sum((tile_start[:, None] >= ends[None, :]).astype(jnp.int32), axis=1), N_EXPERTS - 1)
    tile_info = jnp.concatenate([tile_expert, ends[-1:] // tm]).astype(jnp.int32)
    return src, dest.reshape(n, 2), tile_info


def kernel(x_prompt, x_sample, cache_cmp, cache_slc, cache_win, state_pool, page_table, norm_mix, norm_ffn, norm_kv, norm_final, w_pool, pool_scale, w_kv, w_cmp1, pe_cmp, w_cmp2, w_qg, w_o, w_ffn_in, w_ffn_out, w_router, b_router, w_exp_in, w_exp_out):
    bp, tp, d = x_prompt.shape
    bs, ts, _ = x_sample.shape
    n_p, n_s = bp * tp, bs * ts
    past_len = page_table.shape[1] * PAGE_SIZE
    row = lambda v: v.reshape(1, -1)

    w_pool_b = w_pool[0].astype(BF16)
    w_ffn_in_b, w_ffn_out_b = w_ffn_in.astype(BF16), w_ffn_out.astype(BF16)
    w_exp_in_b, w_exp_out_b = w_exp_in[0].astype(BF16), w_exp_out[0].astype(BF16)
    w_kv_b = w_kv.astype(BF16)
    w_q_b = w_qg[0][:, :N_HEADS * HEAD_DIM].astype(BF16)
    w_g_b = _gate_weights(w_qg[0]).astype(BF16)
    w_o_b = w_o[0].astype(BF16)
    w1_b, pe_b = _compress_weights(w_cmp1, pe_cmp)
    w2_b = w_cmp2.astype(BF16)
    wr = jnp.pad(w_router[0], ((0, 0), (0, LANES - N_EXPERTS)))
    wr_hi = wr.astype(BF16)
    wr_lo = (wr - wr_hi.astype(F32)).astype(BF16)
    b_r = jnp.pad(b_router[0], (0, LANES - N_EXPERTS)).reshape(1, LANES)
    zero_tiles = lambda n, tm: jnp.zeros((n // tm + 1,), jnp.int32).at[n // tm].set(n // tm)

    h1_p, tail_p = _pool_prompt(x_prompt, row(norm_mix[0]), w_pool_b, row(pool_scale[0]))
    h1_s, pool_s = _pool_sample(x_sample, state_pool[0], row(norm_mix[0]), w_pool_b, row(pool_scale[0]))
    h1_p, h1_s = h1_p.reshape(n_p, d), h1_s.reshape(n_s, d)

    tm_p, tm_s = 512, n_s
    h2_p = _ffn(h1_p, row(norm_ffn[0]), w_ffn_in_b, w_ffn_out_b, zero_tiles(n_p, tm_p), norm=True, residual=True, tm=tm_p)
    h2_s = _ffn(h1_s, row(norm_ffn[0]), w_ffn_in_b, w_ffn_out_b, zero_tiles(n_s, tm_s), norm=True, residual=True, tm=tm_s)

    kv_p, kvb_p, q_p, gate_p = _proj(h2_p, row(norm_kv), row(norm_mix[1]), w_kv_b, w_q_b, w_g_b, tm_p)
    kv_s, _, q_s, gate_s = _proj(h2_s, row(norm_kv), row(norm_mix[1]), w_kv_b, w_q_b, w_g_b, tm_s)

    dummy_table = jnp.zeros((1, 1), jnp.int32)
    kv_p3 = kv_p.reshape(bp, tp, KV_WIDTH)
    kvc_p = _compress(
        [kv_p3], [pl.BlockSpec((1, tp, LANES), lambda bi, c, j, tb: (bi, 0, c))], (bp, 4, 1), dummy_table,
        w1_b, pe_b, w2_b, tp, tp // CMP_STRIDE)
    msel_p = _selection_rows(tp // CMP_STRIDE, LANES)
    attn_p = _attn_prompt(q_p.reshape(bp, tp, d), gate_p.reshape(bp, tp, 2 * LANES), kvc_p,
                          kvb_p.reshape(bp, tp, KV_WIDTH), msel_p)

    n_pages = page_table.shape[1]
    pages_step = 16
    cache_cmp_r = cache_cmp.reshape(cache_cmp.shape[0], PAGE_SIZE, BRANCH_WIDTH)
    pages_cmp = 32
    page_spec = lambda pi: pl.BlockSpec(
        (1, PAGE_SIZE, LANES), lambda bi, c, j, tb: (tb[bi, j * pages_cmp + pi], 0, c))
    kvc_s = _compress([cache_cmp_r] * pages_cmp, [page_spec(pi) for pi in range(pages_cmp)],
                      (bs, 4, n_pages // pages_cmp), page_table, w1_b, pe_b, w2_b, PAGE_SIZE, past_len // CMP_STRIDE)
    n_slc_s = past_len // SEL_BLOCK + 1
    sel_width = ((n_slc_s + LANES - 1) // LANES) * LANES
    msel_s = _selection_rows(past_len // CMP_STRIDE, sel_width)
    q_s4 = q_s.reshape(bs, ts, N_KV_HEADS, HEADS_PER_KV, HEAD_DIM).transpose(0, 2, 3, 1, 4)
    q_s4 = jnp.pad(q_s4, ((0, 0), (0, 0), (0, 0), (0, T_PAD - ts), (0, 0)))
    q32 = q_s4.reshape(bs, N_KV_HEADS, HEADS_PER_KV * T_PAD, HEAD_DIM)
    o_cmp_s, sel_s = _attn_sample_cmp(q32, kvc_s, msel_s, past_len, n_slc_s)
    head_mask = jnp.eye(N_KV_HEADS, dtype=BF16)[:, None, :, None]
    q_st = (q32[:, :, :, None, :] * head_mask).reshape(bs, S_ROWS, KV_HALF)
    steps = n_pages // pages_step
    sel_rows = jnp.broadcast_to(sel_s[:, :, None, :, :2 * n_pages], (bs, N_KV_HEADS, HEADS_PER_KV, T_PAD, 2 * n_pages))
    sel_steps = sel_rows.reshape(bs, S_ROWS, steps, 2 * pages_step).transpose(0, 2, 1, 3)
    sel_steps = jnp.pad(sel_steps, ((0, 0), (0, 0), (0, 0), (0, LANES - 2 * pages_step)))
    kv_s3 = kv_s.reshape(bs, ts, N_BRANCH, BRANCH_WIDTH)
    kv_new = kv_s3[:, :, 1:3].transpose(0, 2, 1, 3)
    kv_new = jnp.pad(kv_new, ((0, 0), (0, 0), (0, LANES - ts), (0, 0)))
    gate_s5 = gate_s.reshape(bs, ts, 2, LANES)[..., :2 * HEADS_PER_KV * N_BRANCH]
    gate_s5 = gate_s5.reshape(bs, ts, N_KV_HEADS, HEADS_PER_KV, N_BRANCH).transpose(0, 2, 3, 1, 4)
    gate_rows = jnp.pad(gate_s5, ((0, 0), (0, 0), (0, 0), (0, T_PAD - ts), (0, LANES - N_BRANCH))).reshape(bs, S_ROWS, LANES)
    cache_slc_r = cache_slc.reshape(cache_slc.shape[0], PAGE_SIZE, BRANCH_WIDTH)
    cache_win_r = cache_win.reshape(bs, cache_win.shape[1], BRANCH_WIDTH)
    assert cache_win.shape[1] == WINDOW and ts <= T_PAD
    o_s = _attn_sample(page_table, cache_slc_r, q_st, sel_steps, kv_new, ts, cache_win_r,
                       o_cmp_s.reshape(bs, S_ROWS, HEAD_DIM), gate_rows, pages_step)
    attn_s = o_s.reshape(bs, N_KV_HEADS, HEADS_PER_KV, T_PAD, HEAD_DIM)[:, :, :, :ts].transpose(0, 3, 1, 2, 4).reshape(n_s, d)

    h3_p, u4_p, route_p = _oproj_router(h2_p, attn_p.reshape(n_p, d), w_o_b, row(norm_ffn[1]), wr_hi, wr_lo, b_r, tm_p)
    h3_s, u4_s, route_s = _oproj_router(h2_s, attn_s, w_o_b, row(norm_ffn[1]), wr_hi, wr_lo, b_r, tm_s)
    route = jnp.concatenate([route_p, route_s], axis=0)
    u4 = jnp.concatenate([u4_p, u4_s], axis=0)
    tm_e = 512
    src, dest, tile_expert = _dispatch(route, tm_e)
    y_sorted = _ffn(u4[src], row(norm_ffn[1]), w_exp_in_b, w_exp_out_b, tile_expert, norm=False, residual=False, tm=tm_e)
    y1, y2 = y_sorted[dest[:, 0]], y_sorted[dest[:, 1]]
    y_p = _final(h3_p, y1[:n_p], y2[:n_p], route_p, row(norm_final), tm_p)
    y_s = _final(h3_s, y1[n_p:], y2[n_p:], route_s, row(norm_final), tm_s)

    kv_p5 = kv_p.reshape(bp, tp, N_BRANCH, 2, N_KV_HEADS, HEAD_DIM)
    kv_s5 = kv_s.reshape(bs, ts, N_BRANCH, 2, N_KV_HEADS, HEAD_DIM)
    win_buf = cache_win.shape[1]
    win_sample = jnp.concatenate([cache_win, kv_s5[:, :, 2]], axis=1)[:, -win_buf:]
    return (y_p.reshape(bp, tp, d), y_s.reshape(bs, ts, d),
            kv_p5[:, :, 0], kv_s5[:, :, 0], kv_p5[:, :, 1], kv_s5[:, :, 1],
            kv_p5[:, -min(WINDOW, tp):, 2], win_sample,
            tail_p[None, :, 1:], pool_s[None])
```

```python
import functools

import numpy as np
import jax
import jax.numpy as jnp
from jax import lax
from jax.experimental import pallas as pl
from jax.experimental.pallas import tpu as pltpu

D_MODEL = 1024
POOL_WINDOWS = (2, 4, 8, 16)
POOL_CH = D_MODEL // len(POOL_WINDOWS)
POOL_BUF = max(POOL_WINDOWS) - 1
N_HEADS = 16
HEAD_DIM = 64
N_KV_HEADS = 4
HEADS_PER_KV = 4
N_BRANCH = 3
CMP_BLOCK = 32
CMP_STRIDE = 16
CMP_HIDDEN = 128
SEL_BLOCK = 64
N_SEL = 16
WINDOW = 512
FORCE_LOCAL = 2e4
FORCE_INITIAL = 1e4
HIDE_BLOCK = -1e9
NEG_BIG = -1e30
D_FF = 2816
N_EXPERTS = 8
RMS_EPS = 1e-6
PAGE_SIZE = 128
KV_WIDTH = N_BRANCH * 2 * N_KV_HEADS * HEAD_DIM
BRANCH_WIDTH = 2 * N_KV_HEADS * HEAD_DIM
KV_HALF = N_KV_HEADS * HEAD_DIM
LANES = 128
VMEM_LIMIT = 56 * 1024 * 1024

BF16 = jnp.bfloat16
F32 = jnp.float32


def _params(sem, vmem=VMEM_LIMIT):
    return pltpu.CompilerParams(dimension_semantics=sem, vmem_limit_bytes=vmem)


def _rms(x, g):
    return x * lax.rsqrt(jnp.mean(x * x, axis=-1, keepdims=True) + RMS_EPS) * g


def _dot(a, b):
    return jnp.dot(a, b, preferred_element_type=F32)


def _dot_nt(a, b):
    return lax.dot_general(a, b, (((1,), (1,)), ((), ())), preferred_element_type=F32)


def _split2(x):
    hi = x.astype(BF16)
    return hi, (x - hi.astype(F32)).astype(BF16)


def _dot3(a, b, nt=False):
    dot = _dot_nt if nt else _dot
    a_hi, a_lo = _split2(a)
    b_hi, b_lo = _split2(b)
    return dot(a_hi, b_hi) + dot(a_hi, b_lo) + dot(a_lo, b_hi)


def _mm(a, b, precise):
    return _dot3(a, b) if precise else _dot(a.astype(BF16), b)


def _split3(x):
    hi = x.astype(BF16)
    r1 = x - hi.astype(F32)
    mid = r1.astype(BF16)
    lo = (r1 - mid.astype(F32)).astype(BF16)
    return hi, mid, lo


def _masked_softmax(s, mask):
    s = jnp.where(mask, s, NEG_BIG)
    e = jnp.where(mask, jnp.exp(s - jnp.max(s, axis=-1, keepdims=True)), 0.0)
    return e / jnp.maximum(jnp.sum(e, axis=-1, keepdims=True), 1e-30)


def _topk_mask(score, k):
    lane = lax.broadcasted_iota(jnp.int32, score.shape, 1)
    width = score.shape[1]
    sel = jnp.zeros(score.shape, F32)
    cur = score
    for _ in range(k):
        m = jnp.max(cur, axis=-1, keepdims=True)
        idx = jnp.min(jnp.where(cur == m, lane, width), axis=-1, keepdims=True)
        hit = lane == idx
        sel = jnp.where(hit, 1.0, sel)
        cur = jnp.where(hit, -jnp.inf, cur)
    return sel


def _topk_mask_t(score, k):
    sub = lax.broadcasted_iota(jnp.int32, score.shape, 0)
    height = score.shape[0]
    sel = jnp.zeros(score.shape, F32)
    cur = score
    for _ in range(k):
        m = jnp.max(cur, axis=0, keepdims=True)
        idx = jnp.min(jnp.where(cur == m, sub, height), axis=0, keepdims=True)
        hit = sub == idx
        sel = jnp.where(hit, 1.0, sel)
        cur = jnp.where(hit, -jnp.inf, cur)
    return sel


def _pool_prompt_body(x_ref, xp_ref, g_ref, w_ref, sc_ref, o_ref, tail_ref, ext_ref, *, tt):
    i = pl.program_id(1)
    x = x_ref[0]
    g = g_ref[...]
    u = _rms(x, g)
    up = _rms(xp_ref[0], g) * (i > 0).astype(F32)
    ext_ref[0:16] = up
    ext_ref[16:16 + tt] = u
    pos = i * tt + lax.broadcasted_iota(jnp.int32, (tt, 1), 0)
    for gi, w in enumerate(POOL_WINDOWS):
        ch = slice(gi * POOL_CH, (gi + 1) * POOL_CH)
        s = ext_ref[16:16 + tt, ch]
        for k in range(1, w):
            s = s + ext_ref[16 - k:16 - k + tt, ch]
        cnt = jnp.minimum(pos + 1, w).astype(F32)
        pooled = s / cnt - u[:, ch]
        mixed = _dot(pooled.astype(BF16), w_ref[gi])
        o_ref[0, :, ch] = x[:, ch] + mixed * sc_ref[:, ch]
    tail_ref[0] = ext_ref[tt:tt + 16]


def _pool_prompt(x, g, w_pool, scale, tt=512):
    b, t, d = x.shape
    per16 = tt // 16
    return pl.pallas_call(
        functools.partial(_pool_prompt_body, tt=tt),
        grid=(b, t // tt),
        in_specs=[
            pl.BlockSpec((1, tt, d), lambda bi, i: (bi, i, 0)),
            pl.BlockSpec((1, 16, d), lambda bi, i: (bi, jnp.maximum(i * per16 - 1, 0), 0)),
            pl.BlockSpec((1, d), lambda bi, i: (0, 0)),
            pl.BlockSpec((len(POOL_WINDOWS), POOL_CH, POOL_CH), lambda bi, i: (0, 0, 0)),
            pl.BlockSpec((1, d), lambda bi, i: (0, 0)),
        ],
        out_specs=[
            pl.BlockSpec((1, tt, d), lambda bi, i: (bi, i, 0)),
            pl.BlockSpec((1, 16, d), lambda bi, i: (bi, 0, 0)),
        ],
        out_shape=[jax.ShapeDtypeStruct((b, t, d), F32), jax.ShapeDtypeStruct((b, 16, d), F32)],
        scratch_shapes=[pltpu.VMEM((tt + 16, d), F32)],
        compiler_params=_params(("parallel", "arbitrary")),
        name="pool_prompt",
    )(x, x, g, w_pool, scale)


def _pool_sample_body(x_ref, st_ref, g_ref, w_ref, sc_ref, o_ref, nst_ref, ext_ref, *, t):
    u = _rms(x_ref[0], g_ref[...])
    ext_ref[...] = jnp.zeros(ext_ref.shape, F32)
    ext_ref[0:POOL_BUF] = st_ref[0]
    ext_ref[POOL_BUF:POOL_BUF + t] = u
    rows = 16
    for gi, w in enumerate(POOL_WINDOWS):
        ch = slice(gi * POOL_CH, (gi + 1) * POOL_CH)
        s = ext_ref[POOL_BUF:POOL_BUF + rows, ch]
        for k in range(1, w):
            s = s + ext_ref[POOL_BUF - k:POOL_BUF - k + rows, ch]
        pooled = s / float(w) - ext_ref[POOL_BUF:POOL_BUF + rows, ch]
        mixed = _dot3(pooled, w_ref[gi])
        o_ref[0, :, ch] = x_ref[0, :, ch] + mixed[0:t] * sc_ref[:, ch]
    nst_ref[0] = ext_ref[t:t + POOL_BUF]


def _pool_sample(x, state, g, w_pool, scale):
    b, t, d = x.shape
    return pl.pallas_call(
        functools.partial(_pool_sample_body, t=t),
        grid=(b,),
        in_specs=[
            pl.BlockSpec((1, t, d), lambda bi: (bi, 0, 0)),
            pl.BlockSpec((1, POOL_BUF, d), lambda bi: (bi, 0, 0)),
            pl.BlockSpec((1, d), lambda bi: (0, 0)),
            pl.BlockSpec((len(POOL_WINDOWS), POOL_CH, POOL_CH), lambda bi: (0, 0, 0)),
            pl.BlockSpec((1, d), lambda bi: (0, 0)),
        ],
        out_specs=[
            pl.BlockSpec((1, t, d), lambda bi: (bi, 0, 0)),
            pl.BlockSpec((1, POOL_BUF, d), lambda bi: (bi, 0, 0)),
        ],
        out_shape=[jax.ShapeDtypeStruct((b, t, d), F32), jax.ShapeDtypeStruct((b, POOL_BUF, d), F32)],
        scratch_shapes=[pltpu.VMEM((POOL_BUF + 16 + 1, d), F32)],
        compiler_params=_params(("parallel",)),
        name="pool_sample",
    )(x, state, g, w_pool, scale)


def _ffn_body(te_ref, x_ref, g_ref, wa_ref, wb_ref, wo_ref, o_ref, u_ref, acc_ref, *, norm, residual, precise):
    j = pl.program_id(1)

    @pl.when(j == 0)
    def _():
        x = x_ref[...].astype(F32)
        u = _rms(x, g_ref[...]) if norm else x
        u_ref[...] = u.astype(u_ref.dtype)
        acc_ref[...] = jnp.zeros(acc_ref.shape, F32)

    @pl.when(pl.program_id(0) < te_ref[pl.num_programs(0)])
    def _():
        u = u_ref[...]
        a = _mm(u, wa_ref[0], precise)
        b = _mm(u, wb_ref[0], precise)
        acc_ref[...] += _mm(a * jax.nn.sigmoid(a) * b, wo_ref[0], precise)

    @pl.when(j == pl.num_programs(1) - 1)
    def _():
        y = acc_ref[...]
        if residual:
            y = y + x_ref[...].astype(F32)
        o_ref[...] = y


def _ffn(x, g, w_in, w_out, tile_expert, *, norm, residual, tm, tf=1408, precise=False):
    n, d = x.shape
    nf = D_FF // tf
    grid_spec = pltpu.PrefetchScalarGridSpec(
        num_scalar_prefetch=1,
        grid=(n // tm, nf),
        in_specs=[
            pl.BlockSpec((tm, d), lambda i, j, te: (i, 0)),
            pl.BlockSpec((1, d), lambda i, j, te: (0, 0)),
            pl.BlockSpec((1, d, tf), lambda i, j, te: (te[i], 0, j)),
            pl.BlockSpec((1, d, tf), lambda i, j, te: (te[i], 0, nf + j)),
            pl.BlockSpec((1, tf, d), lambda i, j, te: (te[i], j, 0)),
        ],
        out_specs=pl.BlockSpec((tm, d), lambda i, j, te: (i, 0)),
        scratch_shapes=[pltpu.VMEM((tm, d), F32 if precise else BF16), pltpu.VMEM((tm, d), F32)],
    )
    return pl.pallas_call(
        functools.partial(_ffn_body, norm=norm, residual=residual, precise=precise),
        grid_spec=grid_spec,
        out_shape=jax.ShapeDtypeStruct((n, d), F32),
        compiler_params=_params(("parallel", "arbitrary")),
        name="swiglu",
    )(tile_expert, x, g, w_in, w_in, w_out)


def _proj_body(x_ref, gkv_ref, gq_ref, wkv_ref, wq_ref, wg_ref, kv_ref, kvb_ref, q_ref, gate_ref, *, precise):
    x = x_ref[...]
    kv = _mm(_rms(x, gkv_ref[...]), wkv_ref[...], precise)
    kv_ref[...] = kv
    kvb_ref[...] = kv.astype(BF16)
    uq = _rms(x, gq_ref[...])
    q_ref[...] = (_mm(uq, wq_ref[...], precise) * (HEAD_DIM ** -0.5)).astype(q_ref.dtype)
    gate_ref[...] = jax.nn.sigmoid(_mm(uq, wg_ref[...], precise))


def _proj(x, g_kv, g_q, w_kv, w_q, w_g, tm, precise=False):
    n, d = x.shape
    full = lambda shape: pl.BlockSpec(shape, lambda i: (0,) * len(shape))
    return pl.pallas_call(
        functools.partial(_proj_body, precise=precise),
        grid=(n // tm,),
        in_specs=[pl.BlockSpec((tm, d), lambda i: (i, 0)), full((1, d)), full((1, d)),
                  full(w_kv.shape), full(w_q.shape), full(w_g.shape)],
        out_specs=[pl.BlockSpec((tm, KV_WIDTH), lambda i: (i, 0)), pl.BlockSpec((tm, KV_WIDTH), lambda i: (i, 0)),
                   pl.BlockSpec((tm, d), lambda i: (i, 0)), pl.BlockSpec((tm, 2 * LANES), lambda i: (i, 0))],
        out_shape=[jax.ShapeDtypeStruct((n, KV_WIDTH), F32), jax.ShapeDtypeStruct((n, KV_WIDTH), BF16),
                   jax.ShapeDtypeStruct((n, d), F32 if precise else BF16), jax.ShapeDtypeStruct((n, 2 * LANES), F32)],
        compiler_params=_params(("parallel",)),
        name="kv_q_proj",
    )(x, g_kv, g_q, w_kv, w_q, w_g)


def _compress_body(*refs, n_src, rows_per_src, paged):
    src = refs[1:1 + n_src]
    w1_ref, pe_ref, w2_ref, o_ref, carry_ref = refs[1 + n_src:1 + n_src + 5]
    j = pl.program_id(2)
    n = n_src * rows_per_src // CMP_STRIDE

    @pl.when(j == 0)
    def _():
        carry_ref[...] = jnp.zeros(carry_ref.shape, F32)

    if paged:
        rows_ref = refs[-1]
        for pi, r in enumerate(src):
            rows_ref[pi * rows_per_src:(pi + 1) * rows_per_src, :] = r[0, 0].reshape(LANES, rows_per_src).T
    else:
        rows_ref = src[0].at[0]
    w1 = w1_ref[0]
    pe_out = _mm(pe_ref[0], w1, paged)
    pe_term = pe_out[0:1, 0:CMP_HIDDEN] + pe_out[1:2, CMP_HIDDEN:2 * CMP_HIDDEN]
    row = lax.broadcasted_iota(jnp.int32, (n, 1), 0)
    lhs = jnp.concatenate([rows_ref[pl.ds(s, n, stride=CMP_STRIDE), :] for s in range(CMP_STRIDE)],
                          axis=1)
    p = _mm(lhs, w1, paged)
    for g2 in range(2):
        pa = p[:, g2 * 2 * CMP_HIDDEN:g2 * 2 * CMP_HIDDEN + CMP_HIDDEN]
        pb = p[:, g2 * 2 * CMP_HIDDEN + CMP_HIDDEN:(g2 + 1) * 2 * CMP_HIDDEN]
        pa_prev = jnp.where(row == 0, carry_ref[g2:g2 + 1], pltpu.roll(pa, 1, 0))
        carry_ref[g2:g2 + 1] = pa[n - 1:n]
        hid = jax.nn.gelu(pa_prev + pb + pe_term)
        o_ref[0, 0, g2] = _mm(hid, w2_ref[0], paged)


def _compress(src_arrays, src_specs, grid, table, w1, pe, w2, rows_per_src, n_rows_total, paged):
    n_src = len(src_specs)
    n = n_src * rows_per_src // CMP_STRIDE
    b = grid[0]
    scratch = [pltpu.VMEM((8, CMP_HIDDEN), F32)]
    if paged:
        scratch.append(pltpu.VMEM((n_src * rows_per_src, LANES), F32))
    grid_spec = pltpu.PrefetchScalarGridSpec(
        num_scalar_prefetch=1,
        grid=grid,
        in_specs=list(src_specs) + [
            pl.BlockSpec((1,) + w1.shape[1:], lambda bi, c, j, tb: (c // 2, 0, 0)),
            pl.BlockSpec((1,) + pe.shape[1:], lambda bi, c, j, tb: (c // 2, 0, 0)),
            pl.BlockSpec((1,) + w2.shape[1:], lambda bi, c, j, tb: (c // 2, 0, 0)),
        ],
        out_specs=pl.BlockSpec((1, 1, 2, n, HEAD_DIM), lambda bi, c, j, tb: (bi, c // 2, c % 2, j, 0)),
        scratch_shapes=scratch,
    )
    return pl.pallas_call(
        functools.partial(_compress_body, n_src=n_src, rows_per_src=rows_per_src, paged=paged),
        grid_spec=grid_spec,
        out_shape=jax.ShapeDtypeStruct((b, 2, N_KV_HEADS, n_rows_total, HEAD_DIM), F32),
        compiler_params=_params(("parallel", "parallel", "arbitrary")),
        name="compress",
    )(table, *src_arrays, w1, pe, w2)


def _flash_chunks(q_st, k_ref, v_ref, lane0, c_lo, c_hi, mask_fn, rows):
    def body(c, carry):
        m, l, acc = carry
        k = k_ref[0, pl.ds(pl.multiple_of(c * LANES, LANES), LANES), lane0:lane0 + HEAD_DIM]
        v = v_ref[0, pl.ds(pl.multiple_of(c * LANES, LANES), LANES), lane0:lane0 + HEAD_DIM]
        mask = mask_fn(c)
        s = jnp.where(mask, _dot_nt(q_st, k), NEG_BIG)
        m_new = jnp.maximum(m, jnp.max(s, axis=-1, keepdims=True))
        alpha = jnp.exp(m - m_new)
        p = jnp.where(mask, jnp.exp(s - m_new), 0.0)
        l = alpha * l + jnp.sum(p, axis=-1, keepdims=True)
        acc = alpha * acc + _dot(p.astype(BF16), v)
        return m_new, l, acc

    init = (jnp.full((rows, 1), NEG_BIG, F32), jnp.zeros((rows, 1), F32), jnp.zeros((rows, HEAD_DIM), F32))
    m, l, acc = lax.fori_loop(c_lo, c_hi, body, init)
    return acc / l


def _attn_prompt_body(q_ref, gate_ref, kvc_ref, ks_ref, vs_ref, kw_ref, vw_ref, msel_ref, o_ref, *, tq, n_cmp_rows, n_slc):
    i = pl.program_id(2)
    q0 = i * tq
    rows = HEADS_PER_KV * tq
    tok = lax.broadcasted_iota(jnp.int32, (tq, 1), 0)
    qpos_t = q0 + tok
    qpos = jnp.concatenate([qpos_t] * HEADS_PER_KV, axis=0)
    key_lane = lax.broadcasted_iota(jnp.int32, (1, LANES), 1)
    cmp_row = lax.broadcasted_iota(jnp.int32, (1, n_cmp_rows), 1)
    cmp_vis = (cmp_row >= 1) & (CMP_STRIDE * cmp_row + CMP_STRIDE - 1 <= qpos)
    blk = lax.broadcasted_iota(jnp.int32, (1, LANES), 1)
    e_row = lax.broadcasted_iota(jnp.int32, (LANES, LANES), 0)
    e_col = lax.broadcasted_iota(jnp.int32, (LANES, LANES), 1)
    q_all = q_ref[0]
    gates = gate_ref[0]
    outs = []
    for g2 in range(2):
        q_st = jnp.concatenate(
            [q_all[:, (g2 * HEADS_PER_KV + hh) * HEAD_DIM:(g2 * HEADS_PER_KV + hh + 1) * HEAD_DIM]
             for hh in range(HEADS_PER_KV)], axis=0)
        kc = kvc_ref[0, 0, g2].astype(BF16)
        vc = kvc_ref[0, 1, g2].astype(BF16)
        p_cmp = _masked_softmax(_dot_nt(q_st, kc), cmp_vis)
        o_cmp = _dot(p_cmp.astype(BF16), vc)
        p_sum = p_cmp[0:tq]
        for hh in range(1, HEADS_PER_KV):
            p_sum = p_sum + p_cmp[hh * tq:(hh + 1) * tq]
        hi, mid, lo = _split3(p_sum)
        msel = msel_ref[...]
        imp = _dot(hi, msel) + _dot(mid, msel) + _dot(lo, msel)
        local = blk == jnp.right_shift(qpos_t, 6)
        visible = blk * SEL_BLOCK <= qpos_t
        score = jnp.where(local, FORCE_LOCAL, jnp.where(blk == 0, FORCE_INITIAL, jnp.where(visible, imp, HIDE_BLOCK)))
        score = jnp.where(blk < n_slc, score, -jnp.inf)
        sel = _topk_mask(score, min(N_SEL, n_slc)).astype(BF16)

        def slc_mask(c):
            expand = (e_row == 2 * c + (e_col >= SEL_BLOCK).astype(jnp.int32)).astype(BF16)
            chosen = _dot(sel, expand)
            ok = jnp.where(c * LANES + key_lane <= qpos_t, chosen, 0.0)
            return jnp.concatenate([ok] * HEADS_PER_KV, axis=0) > 0.5

        def win_mask(c):
            kpos = c * LANES + key_lane
            ok = ((kpos <= qpos_t) & (kpos > qpos_t - WINDOW)).astype(F32)
            return jnp.concatenate([ok] * HEADS_PER_KV, axis=0) > 0.5

        lane0 = g2 * HEAD_DIM
        c_hi = (q0 + tq + LANES - 1) // LANES
        o_slc = _flash_chunks(q_st, ks_ref, vs_ref, lane0, 0, c_hi, slc_mask, rows)
        c_lo = jnp.maximum(q0 - WINDOW + 1, 0) // LANES
        o_win = _flash_chunks(q_st, kw_ref, vw_ref, lane0, c_lo, c_hi, win_mask, rows)
        for hh in range(HEADS_PER_KV):
            col = (g2 * HEADS_PER_KV + hh) * N_BRANCH
            rs = slice(hh * tq, (hh + 1) * tq)
            outs.append(gates[:, col:col + 1] * o_cmp[rs] + gates[:, col + 1:col + 2] * o_slc[rs]
                        + gates[:, col + 2:col + 3] * o_win[rs])
    o_ref[0] = jnp.concatenate(outs, axis=1)


KEY_CHUNK = 256


def _flash_keys(q_st, k_ref, v_ref, lane0, c_lo, c_hi, bias_fn, m_ref, l_ref, acc_ref, tq):
    rows = q_st.shape[0]
    m_ref[...] = jnp.full(m_ref.shape, NEG_BIG, F32)
    l_ref[...] = jnp.zeros(l_ref.shape, F32)
    acc_ref[...] = jnp.zeros(acc_ref.shape, F32)

    def body(c, carry):
        start = pl.multiple_of(c * KEY_CHUNK, KEY_CHUNK)
        k = k_ref[0, pl.ds(start, KEY_CHUNK), lane0:lane0 + HEAD_DIM]
        v = v_ref[0, pl.ds(start, KEY_CHUNK), lane0:lane0 + HEAD_DIM]
        s = _dot_nt(q_st, k).reshape(HEADS_PER_KV, tq, KEY_CHUNK) + bias_fn(c)[None]
        s = s.reshape(rows, KEY_CHUNK)
        m_old = m_ref[...]
        m_new = jnp.maximum(m_old, jnp.max(s, axis=-1, keepdims=True))
        alpha = jnp.exp(m_old - m_new)
        p = jnp.exp(s - m_new)
        l_ref[...] = alpha * l_ref[...] + jnp.sum(p, axis=-1, keepdims=True)
        acc_ref[...] = alpha * acc_ref[...] + _dot(p.astype(BF16), v)
        m_ref[...] = m_new
        return carry

    lax.fori_loop(c_lo, c_hi, body, 0)
    return acc_ref[...] / l_ref[...]


def _attn_prompt_body2(q_ref, gate_ref, kvc_ref, ks_ref, vs_ref, kw_ref, vw_ref, msel_ref, o_ref,
                       bias_ref, m_ref, l_ref, acc_ref, *, tq, n_cmp_rows, n_slc):
    i = pl.program_id(2)
    q0 = i * tq
    rows = HEADS_PER_KV * tq
    qpos_t = q0 + lax.broadcasted_iota(jnp.int32, (tq, 1), 0)
    qpos = jnp.concatenate([qpos_t] * HEADS_PER_KV, axis=0)
    key_lane = lax.broadcasted_iota(jnp.int32, (1, KEY_CHUNK), 1)
    cmp_row = lax.broadcasted_iota(jnp.int32, (1, n_cmp_rows), 1)
    cmp_vis = (cmp_row >= 1) & (CMP_STRIDE * cmp_row + CMP_STRIDE - 1 <= qpos)
    blk = lax.broadcasted_iota(jnp.int32, (1, LANES), 1)
    e_row = lax.broadcasted_iota(jnp.int32, (LANES, KEY_CHUNK), 0)
    e_blk = jnp.right_shift(lax.broadcasted_iota(jnp.int32, (LANES, KEY_CHUNK), 1), 6)
    blocks_per_chunk = KEY_CHUNK // SEL_BLOCK
    q_all = q_ref[0]
    gates = gate_ref[0]
    msel = msel_ref[...]
    local = blk == jnp.right_shift(qpos_t, 6)
    visible = blk * SEL_BLOCK <= qpos_t
    c_hi = (q0 + tq + KEY_CHUNK - 1) // KEY_CHUNK
    c_lo = jnp.maximum(q0 - WINDOW + 1, 0) // KEY_CHUNK

    q_sts, o_cmps, scores = [], [], []
    for g2 in range(2):
        q_st = jnp.concatenate(
            [q_all[:, (g2 * HEADS_PER_KV + hh) * HEAD_DIM:(g2 * HEADS_PER_KV + hh + 1) * HEAD_DIM]
             for hh in range(HEADS_PER_KV)], axis=0)
        kc = kvc_ref[0, 0, g2].astype(BF16)
        vc = kvc_ref[0, 1, g2].astype(BF16)
        p_cmp = _masked_softmax(_dot_nt(q_st, kc), cmp_vis)
        o_cmps.append(_dot(p_cmp.astype(BF16), vc))
        p_sum = p_cmp[0:tq]
        for hh in range(1, HEADS_PER_KV):
            p_sum = p_sum + p_cmp[hh * tq:(hh + 1) * tq]
        hi, mid, lo = _split3(p_sum)
        imp = _dot(hi, msel) + _dot(mid, msel) + _dot(lo, msel)
        score = jnp.where(local, FORCE_LOCAL, jnp.where(blk == 0, FORCE_INITIAL, jnp.where(visible, imp, HIDE_BLOCK)))
        scores.append(jnp.where(blk < n_slc, score, -jnp.inf))
        q_sts.append(q_st)
    sel_both = _topk_mask(jnp.concatenate(scores, axis=0), min(N_SEL, n_slc)).astype(BF16)

    def win_bias(c):
        kpos = c * KEY_CHUNK + key_lane
        return jnp.where((kpos <= qpos_t) & (kpos > qpos_t - WINDOW), 0.0, NEG_BIG)

    outs = []
    for g2 in range(2):
        sel = sel_both[g2 * tq:(g2 + 1) * tq]

        def build_bias(c, carry):
            expand = (e_row == blocks_per_chunk * c + e_blk).astype(BF16)
            chosen = _dot(sel, expand)
            ok = (chosen > 0.5) & (c * KEY_CHUNK + key_lane <= qpos_t)
            bias_ref[c] = jnp.where(ok, 0.0, NEG_BIG)
            return carry

        lax.fori_loop(0, c_hi, build_bias, 0)
        lane0 = g2 * HEAD_DIM
        o_slc = _flash_keys(q_sts[g2], ks_ref, vs_ref, lane0, 0, c_hi, lambda c: bias_ref[c], m_ref, l_ref, acc_ref, tq)
        o_win = _flash_keys(q_sts[g2], kw_ref, vw_ref, lane0, c_lo, c_hi, win_bias, m_ref, l_ref, acc_ref, tq)
        for hh in range(HEADS_PER_KV):
            col = (g2 * HEADS_PER_KV + hh) * N_BRANCH
            rs = slice(hh * tq, (hh + 1) * tq)
            outs.append(gates[:, col:col + 1] * o_cmps[g2][rs] + gates[:, col + 1:col + 2] * o_slc[rs]
                        + gates[:, col + 2:col + 3] * o_win[rs])
    o_ref[0] = jnp.concatenate(outs, axis=1)


ONES_ROWS = 16


def _flash_keys_t(k_ref, vt_ref, row0, q_t, c_lo, c_hi, bias_fn, m_ref, acc_ref):
    m_ref[...] = jnp.full(m_ref.shape, NEG_BIG, F32)
    acc_ref[...] = jnp.zeros(acc_ref.shape, F32)
    ones = jnp.ones((ONES_ROWS, KEY_CHUNK), BF16)

    def body(c, carry):
        start = pl.multiple_of(c * KEY_CHUNK, KEY_CHUNK)
        k = k_ref[0, pl.ds(start, KEY_CHUNK), :]
        v_aug = jnp.concatenate([vt_ref[0, c, row0:row0 + HEAD_DIM, :], ones], axis=0)
        bias = bias_fn(c)
        s = _dot(k, q_all) + jnp.concatenate([bias] * HEADS_PER_KV, axis=1)
        m_old = m_ref[...]
        m_new = jnp.maximum(m_old, jnp.max(s, axis=0, keepdims=True))
        alpha = jnp.exp(m_old - m_new)
        p = jnp.exp(s - m_new).astype(BF16)
        acc_ref[...] = alpha * acc_ref[...] + _dot(v_aug, p)
        m_ref[...] = m_new
        return carry

    q_all = jnp.concatenate(q_t, axis=1)
    tq = q_t[0].shape[1]
    lax.fori_loop(c_lo, c_hi, body, 0)
    return [acc_ref[:, hh * tq:(hh + 1) * tq] for hh in range(HEADS_PER_KV)]


def _attn_prompt_body3(q_ref, gate_ref, kvc_ref, ks_ref, vst_ref, kw_ref, vwt_ref, msel_ref, o_ref,
                       bias_ref, m_ref, acc_ref, *, tq, n_cmp_rows, n_slc):
    i = pl.program_id(2)
    q0 = i * tq
    qpos_t = q0 + lax.broadcasted_iota(jnp.int32, (tq, 1), 0)
    qpos = jnp.concatenate([qpos_t] * HEADS_PER_KV, axis=0)
    cmp_row = lax.broadcasted_iota(jnp.int32, (1, n_cmp_rows), 1)
    cmp_vis = (cmp_row >= 1) & (CMP_STRIDE * cmp_row + CMP_STRIDE - 1 <= qpos)
    q_all = q_ref[0]
    gates = gate_ref[0]
    msel_t = msel_ref[...]
    nb = msel_t.shape[0]
    qpos_l = q0 + lax.broadcasted_iota(jnp.int32, (1, tq), 1)
    blk = lax.broadcasted_iota(jnp.int32, (nb, 1), 0)
    local = blk == jnp.right_shift(qpos_l, 6)
    visible = blk * SEL_BLOCK <= qpos_l
    c_hi = (q0 + tq + KEY_CHUNK - 1) // KEY_CHUNK
    c_lo = jnp.maximum(q0 - WINDOW + 1, 0) // KEY_CHUNK

    o_cmps, scores = [], []
    for g2 in range(2):
        q_st = jnp.concatenate(
            [q_all[:, (g2 * HEADS_PER_KV + hh) * HEAD_DIM:(g2 * HEADS_PER_KV + hh + 1) * HEAD_DIM]
             for hh in range(HEADS_PER_KV)], axis=0)
        kc = kvc_ref[0, 0, g2].astype(BF16)
        vc = kvc_ref[0, 1, g2].astype(BF16)
        p_cmp = _masked_softmax(_dot_nt(q_st, kc), cmp_vis)
        o_cmps.append(_dot(p_cmp.astype(BF16), vc))
        p_sum = p_cmp[0:tq]
        for hh in range(1, HEADS_PER_KV):
            p_sum = p_sum + p_cmp[hh * tq:(hh + 1) * tq]
        hi, mid, lo = _split3(p_sum)
        imp = _dot_nt(msel_t, hi) + _dot_nt(msel_t, mid) + _dot_nt(msel_t, lo)
        score = jnp.where(local, FORCE_LOCAL, jnp.where(blk == 0, FORCE_INITIAL, jnp.where(visible, imp, HIDE_BLOCK)))
        scores.append(jnp.where(blk < n_slc, score, -jnp.inf))
    sel_both = _topk_mask_t(jnp.concatenate(scores, axis=1), min(N_SEL, n_slc)).astype(BF16)

    key_sub = lax.broadcasted_iota(jnp.int32, (KEY_CHUNK, 1), 0)
    key_blk = jnp.right_shift(key_sub, 6)
    blk_l = lax.broadcasted_iota(jnp.int32, (1, nb), 1)
    blocks_per_chunk = KEY_CHUNK // SEL_BLOCK

    def win_bias(c):
        kpos = c * KEY_CHUNK + key_sub
        return jnp.where((kpos <= qpos_l) & (kpos > qpos_l - WINDOW), 0.0, NEG_BIG)

    outs = []
    for g2 in range(2):
        sel_t = sel_both[:, g2 * tq:(g2 + 1) * tq]

        def build_bias(c, carry):
            expand = (blocks_per_chunk * c + key_blk == blk_l).astype(BF16)
            ok = (_dot(expand, sel_t) > 0.5) & (c * KEY_CHUNK + key_sub <= qpos_l)
            bias_ref[c] = jnp.where(ok, 0.0, NEG_BIG)
            return carry

        lax.fori_loop(0, c_hi, build_bias, 0)
        q_g = q_all[:, g2 * KV_HALF:(g2 + 1) * KV_HALF].astype(F32).T.astype(BF16)
        zeros = jnp.zeros((HEAD_DIM, tq), BF16)
        q_t = []
        for hh in range(HEADS_PER_KV):
            q_h = q_g[hh * HEAD_DIM:(hh + 1) * HEAD_DIM]
            q_t.append(jnp.concatenate([q_h, zeros] if g2 == 0 else [zeros, q_h], axis=0))
        row0 = g2 * HEAD_DIM
        normalise = lambda a: (a[0:HEAD_DIM] / a[HEAD_DIM:HEAD_DIM + 1]).T
        o_slc = [normalise(a) for a in
                 _flash_keys_t(ks_ref, vst_ref, row0, q_t, 0, c_hi, lambda c: bias_ref[c], m_ref, acc_ref)]
        o_win = [normalise(a) for a in _flash_keys_t(kw_ref, vwt_ref, row0, q_t, c_lo, c_hi, win_bias, m_ref, acc_ref)]
        for hh in range(HEADS_PER_KV):
            col = (g2 * HEADS_PER_KV + hh) * N_BRANCH
            outs.append(gates[:, col:col + 1] * o_cmps[g2][hh * tq:(hh + 1) * tq] + gates[:, col + 1:col + 2] * o_slc[hh]
                        + gates[:, col + 2:col + 3] * o_win[hh])
    o_ref[0] = jnp.concatenate(outs, axis=1)


def _attn_prompt(q, gates, kvc, kvb, msel, tq=256):
    b, t, d = q.shape
    r = kvc.shape[3]
    half = 2 * HEAD_DIM
    n_chunks = t // KEY_CHUNK
    k_spec = lambda base: pl.BlockSpec((1, t, half), lambda bi, gp, i: (bi, 0, base + gp))
    vt_spec = lambda base: pl.BlockSpec((1, n_chunks, half, KEY_CHUNK), lambda bi, gp, i: (bi, 0, base + gp, 0))
    slc0 = BRANCH_WIDTH // half
    win0 = 2 * BRANCH_WIDTH // half
    vofs = KV_HALF // half
    kv_t = kvb.reshape(b, n_chunks, KEY_CHUNK, KV_WIDTH).transpose(0, 1, 3, 2)
    return pl.pallas_call(
        functools.partial(_attn_prompt_body3, tq=tq, n_cmp_rows=r, n_slc=t // SEL_BLOCK),
        grid=(b, 2, t // tq),
        scratch_shapes=[pltpu.VMEM((n_chunks, KEY_CHUNK, tq), F32),
                        pltpu.VMEM((1, HEADS_PER_KV * tq), F32),
                        pltpu.VMEM((HEAD_DIM + ONES_ROWS, HEADS_PER_KV * tq), F32)],
        in_specs=[
            pl.BlockSpec((1, tq, 2 * HEADS_PER_KV * HEAD_DIM), lambda bi, gp, i: (bi, i, gp)),
            pl.BlockSpec((1, tq, LANES), lambda bi, gp, i: (bi, i, gp)),
            pl.BlockSpec((1, 2, 2, r, HEAD_DIM), lambda bi, gp, i: (bi, 0, gp, 0, 0)),
            k_spec(slc0), vt_spec(slc0 + vofs), k_spec(win0), vt_spec(win0 + vofs),
            pl.BlockSpec(msel.shape, lambda bi, gp, i: (0, 0)),
        ],
        out_specs=pl.BlockSpec((1, tq, 2 * HEADS_PER_KV * HEAD_DIM), lambda bi, gp, i: (bi, i, gp)),
        out_shape=jax.ShapeDtypeStruct((b, t, d), F32),
        compiler_params=_params(("parallel", "parallel", "arbitrary")),
        name="attn_prompt",
    )(q, gates, kvc, kvb, kv_t, kvb, kv_t, msel)


T_PAD = 8
S_ROWS = N_KV_HEADS * HEADS_PER_KV * T_PAD


def _attn_sample_cmp_body(q_ref, kvc_ref, msel_ref, o_ref, sel_ref, *, past_len, n_cmp_rows, n_slc):
    tok = lax.broadcasted_iota(jnp.int32, (T_PAD, 1), 0)
    qpos_t = past_len + tok
    qpos = jnp.concatenate([qpos_t] * HEADS_PER_KV, axis=0)
    cmp_row = lax.broadcasted_iota(jnp.int32, (1, n_cmp_rows), 1)
    cmp_vis = (cmp_row >= 1) & (CMP_STRIDE * cmp_row + CMP_STRIDE - 1 <= qpos)
    blk = lax.broadcasted_iota(jnp.int32, (1, msel_ref.shape[1]), 1)
    msel = msel_ref[...]
    scores = []
    for g in range(N_KV_HEADS):
        q_st = q_ref[0, g]
        p_cmp = _masked_softmax(_dot3(q_st, kvc_ref[0, 0, g], nt=True), cmp_vis)
        o_ref[0, g] = _dot3(p_cmp, kvc_ref[0, 1, g])
        p_sum = p_cmp[0:T_PAD]
        for hh in range(1, HEADS_PER_KV):
            p_sum = p_sum + p_cmp[hh * T_PAD:(hh + 1) * T_PAD]
        hi, mid, lo = _split3(p_sum)
        imp = _dot(hi, msel) + _dot(mid, msel) + _dot(lo, msel)
        local = blk == jnp.right_shift(qpos_t, 6)
        visible = blk * SEL_BLOCK <= qpos_t
        score = jnp.where(local, FORCE_LOCAL, jnp.where(blk == 0, FORCE_INITIAL, jnp.where(visible, imp, HIDE_BLOCK)))
        scores.append(jnp.where(blk < n_slc, score, -jnp.inf))
    sel = _topk_mask(jnp.concatenate(scores, axis=0), min(N_SEL, n_slc))
    for g in range(N_KV_HEADS):
        sel_ref[0, g] = sel[g * T_PAD:(g + 1) * T_PAD]


def _attn_sample_cmp(q32, kvc, msel, past_len, n_slc):
    b = q32.shape[0]
    r = kvc.shape[3]
    w = msel.shape[1]
    return pl.pallas_call(
        functools.partial(_attn_sample_cmp_body, past_len=past_len, n_cmp_rows=r, n_slc=n_slc),
        grid=(b,),
        in_specs=[
            pl.BlockSpec((1, N_KV_HEADS, HEADS_PER_KV * T_PAD, HEAD_DIM), lambda bi: (bi, 0, 0, 0)),
            pl.BlockSpec((1, 2, N_KV_HEADS, r, HEAD_DIM), lambda bi: (bi, 0, 0, 0, 0)),
            pl.BlockSpec(msel.shape, lambda bi: (0, 0)),
        ],
        out_specs=[
            pl.BlockSpec((1, N_KV_HEADS, HEADS_PER_KV * T_PAD, HEAD_DIM), lambda bi: (bi, 0, 0, 0)),
            pl.BlockSpec((1, N_KV_HEADS, T_PAD, w), lambda bi: (bi, 0, 0, 0)),
        ],
        out_shape=[jax.ShapeDtypeStruct((b, N_KV_HEADS, HEADS_PER_KV * T_PAD, HEAD_DIM), F32),
                   jax.ShapeDtypeStruct((b, N_KV_HEADS, T_PAD, w), F32)],
        compiler_params=_params(("parallel",)),
        name="attn_sample_cmp",
    )(q32, kvc, msel)


def _attn_sample_body(*refs, n_pages, t_new):
    pages = refs[1:1 + n_pages]
    (q_ref, sel_ref, new_ref, win_ref, ocmp_ref, gate_ref, o_ref, m_ref, l_ref, acc_ref) = refs[1 + n_pages:]
    j = pl.program_id(1)
    row = lax.broadcasted_iota(jnp.int32, (S_ROWS, 1), 0)
    tok = jnp.bitwise_and(row, T_PAD - 1)
    row_g = jnp.right_shift(row, 5)
    lane = lax.broadcasted_iota(jnp.int32, (1, LANES), 1)
    q = q_ref[0]

    @pl.when(j == 0)
    def _():
        m_ref[...] = jnp.full(m_ref.shape, NEG_BIG, F32)
        l_ref[...] = jnp.zeros(l_ref.shape, F32)
        acc_ref[...] = jnp.zeros(acc_ref.shape, F32)

    sel = sel_ref[0, 0]
    s_parts, m_parts = [], []
    for pi in range(n_pages):
        k_t = pages[pi][0, 0].reshape(KV_HALF, PAGE_SIZE)
        s_parts.append(_dot3(q, k_t))
        chosen = jnp.where(lane < SEL_BLOCK, sel[:, 2 * pi:2 * pi + 1], sel[:, 2 * pi + 1:2 * pi + 2])
        m_parts.append(chosen > 0.5)
    mask = jnp.concatenate(m_parts, axis=1)
    s = jnp.where(mask, jnp.concatenate(s_parts, axis=1), NEG_BIG)
    m_old = m_ref[...]
    m_new = jnp.maximum(m_old, jnp.max(s, axis=-1, keepdims=True))
    alpha = jnp.exp(m_old - m_new)
    p = jnp.where(mask, jnp.exp(s - m_new), 0.0)
    l_ref[...] = alpha * l_ref[...] + jnp.sum(p, axis=-1, keepdims=True)
    pv = jnp.zeros((S_ROWS, KV_HALF), F32)
    for pi in range(n_pages):
        v_t = pages[pi][0, 1].reshape(KV_HALF, PAGE_SIZE)
        pv = pv + _dot3(p[:, pi * LANES:(pi + 1) * LANES], v_t, nt=True)
    acc_ref[...] = alpha * acc_ref[...] + pv
    m_ref[...] = m_new

    @pl.when(j == pl.num_programs(1) - 1)
    def _():
        def own_head(x):
            out = jnp.zeros((S_ROWS, HEAD_DIM), F32)
            for g in range(N_KV_HEADS):
                out = out + jnp.where(row_g == g, x[:, g * HEAD_DIM:(g + 1) * HEAD_DIM], 0.0)
            return out

        new_ok = (lane <= tok) & (lane < t_new)
        k_new = new_ref[0, 0, :, 0:KV_HALF]
        v_new = new_ref[0, 0, :, KV_HALF:2 * KV_HALF]
        s_n = jnp.where(new_ok, _dot3(q, k_new, nt=True), NEG_BIG)
        m_old = m_ref[...]
        m_fin = jnp.maximum(m_old, jnp.max(s_n, axis=-1, keepdims=True))
        alpha = jnp.exp(m_old - m_fin)
        p_n = jnp.where(new_ok, jnp.exp(s_n - m_fin), 0.0)
        l_fin = alpha * l_ref[...] + jnp.sum(p_n, axis=-1, keepdims=True)
        o_slc = own_head((alpha * acc_ref[...] + _dot3(p_n, v_new)) / l_fin)
        nw = win_ref.shape[-1]
        kw_t = win_ref[0, 0].reshape(KV_HALF, nw)
        vw_t = win_ref[0, 1].reshape(KV_HALF, nw)
        kw_new = new_ref[0, 1, :, 0:KV_HALF]
        vw_new = new_ref[0, 1, :, KV_HALF:2 * KV_HALF]
        widx = lax.broadcasted_iota(jnp.int32, (1, nw), 1)
        w_mask = jnp.concatenate([jnp.broadcast_to(widx > tok, (S_ROWS, nw)),
                                  jnp.broadcast_to(new_ok, (S_ROWS, LANES))], axis=1)
        s_w = jnp.concatenate([_dot3(q, kw_t), _dot3(q, kw_new, nt=True)], axis=1)
        p_w = _masked_softmax(s_w, w_mask)
        o_win = own_head(_dot3(p_w[:, 0:nw], vw_t, nt=True) + _dot3(p_w[:, nw:], vw_new))
        gates = gate_ref[0]
        o_ref[0] = gates[:, 0:1] * ocmp_ref[0] + gates[:, 1:2] * o_slc + gates[:, 2:3] * o_win


def _attn_sample(page_table, cache_slc, q_st, sel_steps, kv_new, t_new, cache_win, o_cmp, gate_rows, n_pages_step):
    b, n_pages = page_table.shape
    steps = n_pages // n_pages_step
    page_spec = lambda pi: pl.BlockSpec(
        (1, 2, N_KV_HEADS, HEAD_DIM, PAGE_SIZE), lambda bi, j, pt: (pt[bi, j * n_pages_step + pi], 0, 0, 0, 0))
    grid_spec = pltpu.PrefetchScalarGridSpec(
        num_scalar_prefetch=1,
        grid=(b, steps),
        in_specs=[page_spec(pi) for pi in range(n_pages_step)] + [
            pl.BlockSpec((1, S_ROWS, KV_HALF), lambda bi, j, pt: (bi, 0, 0)),
            pl.BlockSpec((1, 1, S_ROWS, LANES), lambda bi, j, pt: (bi, j, 0, 0)),
            pl.BlockSpec((1, 2, LANES, BRANCH_WIDTH), lambda bi, j, pt: (bi, 0, 0, 0)),
            pl.BlockSpec((1,) + cache_win.shape[1:], lambda bi, j, pt: (bi, 0, 0, 0, 0)),
            pl.BlockSpec((1, S_ROWS, HEAD_DIM), lambda bi, j, pt: (bi, 0, 0)),
            pl.BlockSpec((1, S_ROWS, LANES), lambda bi, j, pt: (bi, 0, 0)),
        ],
        out_specs=pl.BlockSpec((1, S_ROWS, HEAD_DIM), lambda bi, j, pt: (bi, 0, 0)),
        scratch_shapes=[pltpu.VMEM((S_ROWS, 1), F32), pltpu.VMEM((S_ROWS, 1), F32), pltpu.VMEM((S_ROWS, KV_HALF), F32)],
    )
    return pl.pallas_call(
        functools.partial(_attn_sample_body, n_pages=n_pages_step, t_new=t_new),
        grid_spec=grid_spec,
        out_shape=jax.ShapeDtypeStruct((b, S_ROWS, HEAD_DIM), F32),
        compiler_params=_params(("parallel", "arbitrary")),
        name="attn_sample",
    )(page_table, *([cache_slc] * n_pages_step), q_st, sel_steps, kv_new, cache_win, o_cmp, gate_rows)


def _oproj_router_body(h_ref, a_ref, wo_ref, g_ref, wr_hi_ref, wr_lo_ref, br_ref, h3_ref, u_ref, route_ref, *, precise):
    h3 = h_ref[...] + _mm(a_ref[...], wo_ref[...], precise)
    h3_ref[...] = h3
    u = _rms(h3, g_ref[...])
    u_ref[...] = u.astype(BF16)
    u_hi = u.astype(BF16)
    u_lo = (u - u_hi.astype(F32)).astype(BF16)
    logits = _dot(u_hi, wr_hi_ref[...]) + _dot(u_hi, wr_lo_ref[...]) + _dot(u_lo, wr_hi_ref[...]) + br_ref[...]
    lane = lax.broadcasted_iota(jnp.int32, logits.shape, 1)
    logits = jnp.where(lane < N_EXPERTS, logits, -jnp.inf)
    v1 = jnp.max(logits, axis=-1, keepdims=True)
    i1 = jnp.min(jnp.where(logits == v1, lane, LANES), axis=-1, keepdims=True)
    rest = jnp.where(lane == i1, -jnp.inf, logits)
    v2 = jnp.max(rest, axis=-1, keepdims=True)
    i2 = jnp.min(jnp.where(rest == v2, lane, LANES), axis=-1, keepdims=True)
    e2 = jnp.exp(v2 - v1)
    w1 = 1.0 / (1.0 + e2)
    w2 = e2 / (1.0 + e2)
    route_ref[...] = jnp.where(lane == 0, i1.astype(F32), jnp.where(lane == 1, i2.astype(F32),
                               jnp.where(lane == 2, w1, jnp.where(lane == 3, w2, 0.0))))


def _oproj_router(h, attn, w_o, g, wr_hi, wr_lo, b_r, tm, precise=False):
    n, d = h.shape
    full = lambda shape: pl.BlockSpec(shape, lambda i: (0,) * len(shape))
    tile = lambda w: pl.BlockSpec((tm, w), lambda i: (i, 0))
    return pl.pallas_call(
        functools.partial(_oproj_router_body, precise=precise),
        grid=(n // tm,),
        in_specs=[tile(d), tile(d), full(w_o.shape), full((1, d)), full(wr_hi.shape), full(wr_lo.shape), full(b_r.shape)],
        out_specs=[tile(d), tile(d), tile(LANES)],
        out_shape=[jax.ShapeDtypeStruct((n, d), F32), jax.ShapeDtypeStruct((n, d), BF16),
                   jax.ShapeDtypeStruct((n, LANES), F32)],
        compiler_params=_params(("parallel",)),
        name="oproj_router",
    )(h, attn, w_o, g, wr_hi, wr_lo, b_r)


def _final_body(h_ref, y1_ref, y2_ref, route_ref, g_ref, o_ref):
    r = route_ref[...]
    h4 = h_ref[...] + (r[:, 2:3] * y1_ref[...] + r[:, 3:4] * y2_ref[...])
    o_ref[...] = _rms(h4, g_ref[...])


def _final(h, y1, y2, route, g, tm):
    n, d = h.shape
    tile = lambda w: pl.BlockSpec((tm, w), lambda i: (i, 0))
    return pl.pallas_call(
        _final_body,
        grid=(n // tm,),
        in_specs=[tile(d), tile(d), tile(d), tile(LANES), pl.BlockSpec((1, d), lambda i: (0, 0))],
        out_specs=tile(d),
        out_shape=jax.ShapeDtypeStruct((n, d), F32),
        compiler_params=_params(("parallel",)),
        name="moe_combine_norm",
    )(h, y1, y2, route, g)


def _selection_rows(n_rows, width):
    r = np.arange(n_rows)[:, None]
    j = np.arange(width)[None, :]
    i = r - 1
    ov = (r >= 1) & (CMP_STRIDE * i < SEL_BLOCK * (j + 1)) & (CMP_STRIDE * i + CMP_BLOCK > SEL_BLOCK * j)
    return jnp.asarray(ov.astype(np.float32), dtype=BF16)


def _compress_weights(w_cmp1, pe_cmp):
    w = w_cmp1.reshape(2, 2, CMP_STRIDE, HEAD_DIM, CMP_HIDDEN).transpose(0, 2, 3, 1, 4)
    eye = jnp.eye(2, dtype=w.dtype)
    w1 = jnp.einsum('asdvh,gk->asgdkvh', w, eye).reshape(2, CMP_STRIDE * 2 * HEAD_DIM, 4 * CMP_HIDDEN)
    pe = pe_cmp.reshape(2, 2, CMP_STRIDE, 1, HEAD_DIM)
    pe = jnp.concatenate([pe, jnp.zeros_like(pe)], axis=3).reshape(2, 2, CMP_STRIDE * 2 * HEAD_DIM)
    pe = jnp.concatenate([pe, jnp.zeros((2, 6, pe.shape[-1]), pe.dtype)], axis=1)
    return w1, pe


def _gate_weights(w_qg_l):
    wg = w_qg_l[:, N_HEADS * HEAD_DIM:].reshape(D_MODEL, 2, 2 * HEADS_PER_KV * N_BRANCH)
    wg = jnp.pad(wg, ((0, 0), (0, 0), (0, LANES - wg.shape[-1])))
    return wg.reshape(D_MODEL, 2 * LANES)


def _dispatch(route, tm):
    n = route.shape[0]
    e_idx = route[:, 0:2].astype(jnp.int32)
    flat = e_idx.reshape(-1)
    onehot = (flat[:, None] == jnp.arange(N_EXPERTS)[None, :]).astype(jnp.int32)
    counts = jnp.sum(onehot, axis=0)
    padded = ((counts + tm - 1) // tm) * tm
    starts = jnp.cumsum(padded) - padded
    rank = jnp.take_along_axis(jnp.cumsum(onehot, axis=0), flat[:, None], axis=1)[:, 0] - 1
    dest = starts[flat] + rank
    n_tiles = -(-2 * n // tm) + N_EXPERTS
    src = jnp.zeros((n_tiles * tm,), jnp.int32).at[dest].set(jnp.arange(2 * n, dtype=jnp.int32) // 2)
    tile_start = jnp.arange(n_tiles, dtype=jnp.int32) * tm
    ends = jnp.cumsum(padded)
    tile_expert = jnp.minimum(jnp.sum((tile_start[:, None] >= ends[None, :]).astype(jnp.int32), axis=1), N_EXPERTS - 1)
    tile_info = jnp.concatenate([tile_expert, ends[-1:] // tm]).astype(jnp.int32)
    return src, dest.reshape(n, 2), tile_info


def kernel(x_prompt, x_sample, cache_cmp, cache_slc, cache_win, state_pool, page_table, norm_mix, norm_ffn, norm_kv, norm_final, w_pool, pool_scale, w_kv, w_cmp1, pe_cmp, w_cmp2, w_qg, w_o, w_ffn_in, w_ffn_out, w_router, b_router, w_exp_in, w_exp_out):
    bp, tp, d = x_prompt.shape
    bs, ts, _ = x_sample.shape
    n_p, n_s = bp * tp, bs * ts
    past_len = page_table.shape[1] * PAGE_SIZE
    row = lambda v: v.reshape(1, -1)

    w_pool_b = w_pool[0].astype(BF16)
    w_ffn_in_b, w_ffn_out_b = w_ffn_in.astype(BF16), w_ffn_out.astype(BF16)
    w_exp_in_b, w_exp_out_b = w_exp_in[0].astype(BF16), w_exp_out[0].astype(BF16)
    w_kv_b = w_kv.astype(BF16)
    w_q_b = w_qg[0][:, :N_HEADS * HEAD_DIM].astype(BF16)
    w_g_b = _gate_weights(w_qg[0]).astype(BF16)
    w_o_b = w_o[0].astype(BF16)
    w1_f, pe_f = _compress_weights(w_cmp1, pe_cmp)
    w1_b, pe_b, w2_b = w1_f.astype(BF16), pe_f.astype(BF16), w_cmp2.astype(BF16)
    wr = jnp.pad(w_router[0], ((0, 0), (0, LANES - N_EXPERTS)))
    wr_hi = wr.astype(BF16)
    wr_lo = (wr - wr_hi.astype(F32)).astype(BF16)
    b_r = jnp.pad(b_router[0], (0, LANES - N_EXPERTS)).reshape(1, LANES)
    zero_tiles = lambda n, tm: jnp.zeros((n // tm + 1,), jnp.int32).at[n // tm].set(n // tm)

    h1_p, tail_p = _pool_prompt(x_prompt, row(norm_mix[0]), w_pool_b, row(pool_scale[0]))
    h1_s, pool_s = _pool_sample(x_sample, state_pool[0], row(norm_mix[0]), w_pool[0], row(pool_scale[0]))
    h1_p, h1_s = h1_p.reshape(n_p, d), h1_s.reshape(n_s, d)

    tm_p, tm_s = 512, n_s
    h2_p = _ffn(h1_p, row(norm_ffn[0]), w_ffn_in_b, w_ffn_out_b, zero_tiles(n_p, tm_p), norm=True, residual=True, tm=tm_p)
    h2_s = _ffn(h1_s, row(norm_ffn[0]), w_ffn_in, w_ffn_out, zero_tiles(n_s, tm_s), norm=True, residual=True, tm=tm_s,
                tf=256, precise=True)

    kv_p, kvb_p, q_p, gate_p = _proj(h2_p, row(norm_kv), row(norm_mix[1]), w_kv_b, w_q_b, w_g_b, tm_p)
    kv_s, _, q_s, gate_s = _proj(h2_s, row(norm_kv), row(norm_mix[1]), w_kv, w_qg[0][:, :N_HEADS * HEAD_DIM],
                                 _gate_weights(w_qg[0]), tm_s, precise=True)

    dummy_table = jnp.zeros((1, 1), jnp.int32)
    kv_p3 = kv_p.reshape(bp, tp, KV_WIDTH)
    kvc_p = _compress(
        [kv_p3], [pl.BlockSpec((1, tp, LANES), lambda bi, c, j, tb: (bi, 0, c))], (bp, 4, 1), dummy_table,
        w1_b, pe_b, w2_b, tp, tp // CMP_STRIDE, paged=False)
    msel_p = _selection_rows(tp // CMP_STRIDE, -(-(tp // SEL_BLOCK) // 8) * 8).T
    attn_p = _attn_prompt(q_p.reshape(bp, tp, d), gate_p.reshape(bp, tp, 2 * LANES), kvc_p,
                          kvb_p.reshape(bp, tp, KV_WIDTH), msel_p)

    n_pages = page_table.shape[1]
    pages_step = 16
    cache_cmp_t = cache_cmp.transpose(0, 2, 3, 4, 1)
    cache_slc_t = cache_slc.transpose(0, 2, 3, 4, 1)
    cache_win_t = cache_win.transpose(0, 2, 3, 4, 1)
    pages_cmp = 32
    page_spec = lambda pi: pl.BlockSpec(
        (1, 1, 2, HEAD_DIM, PAGE_SIZE), lambda bi, c, j, tb: (tb[bi, j * pages_cmp + pi], c // 2, c % 2, 0, 0))
    kvc_s = _compress([cache_cmp_t] * pages_cmp, [page_spec(pi) for pi in range(pages_cmp)],
                      (bs, 4, n_pages // pages_cmp), page_table, w1_f, pe_f, w_cmp2, PAGE_SIZE, past_len // CMP_STRIDE,
                      paged=True)
    n_slc_s = past_len // SEL_BLOCK + 1
    sel_width = ((n_slc_s + LANES - 1) // LANES) * LANES
    msel_s = _selection_rows(past_len // CMP_STRIDE, sel_width)
    q_s4 = q_s.reshape(bs, ts, N_KV_HEADS, HEADS_PER_KV, HEAD_DIM).transpose(0, 2, 3, 1, 4)
    q_s4 = jnp.pad(q_s4, ((0, 0), (0, 0), (0, 0), (0, T_PAD - ts), (0, 0)))
    q32 = q_s4.reshape(bs, N_KV_HEADS, HEADS_PER_KV * T_PAD, HEAD_DIM)
    o_cmp_s, sel_s = _attn_sample_cmp(q32, kvc_s, msel_s, past_len, n_slc_s)
    head_mask = jnp.eye(N_KV_HEADS, dtype=F32)[:, None, :, None]
    q_st = (q32[:, :, :, None, :] * head_mask).reshape(bs, S_ROWS, KV_HALF)
    steps = n_pages // pages_step
    sel_rows = jnp.broadcast_to(sel_s[:, :, None, :, :2 * n_pages], (bs, N_KV_HEADS, HEADS_PER_KV, T_PAD, 2 * n_pages))
    sel_steps = sel_rows.reshape(bs, S_ROWS, steps, 2 * pages_step).transpose(0, 2, 1, 3)
    sel_steps = jnp.pad(sel_steps, ((0, 0), (0, 0), (0, 0), (0, LANES - 2 * pages_step)))
    kv_s3 = kv_s.reshape(bs, ts, N_BRANCH, BRANCH_WIDTH)
    kv_new = kv_s3[:, :, 1:3].transpose(0, 2, 1, 3)
    kv_new = jnp.pad(kv_new, ((0, 0), (0, 0), (0, LANES - ts), (0, 0)))
    gate_s5 = gate_s.reshape(bs, ts, 2, LANES)[..., :2 * HEADS_PER_KV * N_BRANCH]
    gate_s5 = gate_s5.reshape(bs, ts, N_KV_HEADS, HEADS_PER_KV, N_BRANCH).transpose(0, 2, 3, 1, 4)
    gate_rows = jnp.pad(gate_s5, ((0, 0), (0, 0), (0, 0), (0, T_PAD - ts), (0, LANES - N_BRANCH))).reshape(bs, S_ROWS, LANES)
    assert cache_win.shape[1] == WINDOW and ts <= T_PAD
    o_s = _attn_sample(page_table, cache_slc_t, q_st, sel_steps, kv_new, ts, cache_win_t,
                       o_cmp_s.reshape(bs, S_ROWS, HEAD_DIM), gate_rows, pages_step)
    attn_s = o_s.reshape(bs, N_KV_HEADS, HEADS_PER_KV, T_PAD, HEAD_DIM)[:, :, :, :ts].transpose(0, 3, 1, 2, 4).reshape(n_s, d)

    h3_p, u4_p, route_p = _oproj_router(h2_p, attn_p.reshape(n_p, d), w_o_b, row(norm_ffn[1]), wr_hi, wr_lo, b_r, tm_p)
    h3_s, u4_s, route_s = _oproj_router(h2_s, attn_s, w_o[0], row(norm_ffn[1]), wr_hi, wr_lo, b_r, tm_s, precise=True)
    route = jnp.concatenate([route_p, route_s], axis=0)
    u4 = jnp.concatenate([u4_p, u4_s], axis=0)
    tm_e = 512
    src, dest, tile_expert = _dispatch(route, tm_e)
    y_sorted = _ffn(u4[src], row(norm_ffn[1]), w_exp_in_b, w_exp_out_b, tile_expert, norm=False, residual=False, tm=tm_e)
    y1, y2 = y_sorted[dest[:, 0]], y_sorted[dest[:, 1]]
    y_p = _final(h3_p, y1[:n_p], y2[:n_p], route_p, row(norm_final), tm_p)
    y_s = _final(h3_s, y1[n_p:], y2[n_p:], route_s, row(norm_final), tm_s)

    kv_p5 = kv_p.reshape(bp, tp, N_BRANCH, 2, N_KV_HEADS, HEAD_DIM)
    kv_s5 = kv_s.reshape(bs, ts, N_BRANCH, 2, N_KV_HEADS, HEAD_DIM)
    win_buf = cache_win.shape[1]
    win_sample = jnp.concatenate([cache_win, kv_s5[:, :, 2]], axis=1)[:, -win_buf:]
    return (y_p.reshape(bp, tp, d), y_s.reshape(bs, ts, d),
            kv_p5[:, :, 0], kv_s5[:, :, 0], kv_p5[:, :, 1], kv_s5[:, :, 1],
            kv_p5[:, -min(WINDOW, tp):, 2], win_sample,
            tail_p[None, :, 1:], pool_s[None])
```

```python
import functools

import numpy as np
import jax
import jax.numpy as jnp
from jax import lax
from jax.experimental import pallas as pl
from jax.experimental.pallas import tpu as pltpu

D_MODEL = 1024
POOL_WINDOWS = (2, 4, 8, 16)
POOL_CH = D_MODEL // len(POOL_WINDOWS)
POOL_BUF = max(POOL_WINDOWS) - 1
N_HEADS = 16
HEAD_DIM = 64
N_KV_HEADS = 4
HEADS_PER_KV = 4
N_BRANCH = 3
CMP_BLOCK = 32
CMP_STRIDE = 16
CMP_HIDDEN = 128
SEL_BLOCK = 64
N_SEL = 16
WINDOW = 512
FORCE_LOCAL = 2e4
FORCE_INITIAL = 1e4
HIDE_BLOCK = -1e9
NEG_BIG = -1e30
D_FF = 2816
N_EXPERTS = 8
RMS_EPS = 1e-6
PAGE_SIZE = 128
KV_WIDTH = N_BRANCH * 2 * N_KV_HEADS * HEAD_DIM
BRANCH_WIDTH = 2 * N_KV_HEADS * HEAD_DIM
KV_HALF = N_KV_HEADS * HEAD_DIM
LANES = 128
VMEM_LIMIT = 56 * 1024 * 1024

BF16 = jnp.bfloat16
F32 = jnp.float32


def _params(sem, vmem=VMEM_LIMIT):
    return pltpu.CompilerParams(dimension_semantics=sem, vmem_limit_bytes=vmem)


def _rms(x, g):
    return x * lax.rsqrt(jnp.mean(x * x, axis=-1, keepdims=True) + RMS_EPS) * g


def _dot(a, b):
    return jnp.dot(a, b, preferred_element_type=F32)


def _dot_nt(a, b):
    return lax.dot_general(a, b, (((1,), (1,)), ((), ())), preferred_element_type=F32)


def _split2(x):
    hi = x.astype(BF16)
    return hi, (x - hi.astype(F32)).astype(BF16)


def _dot3(a, b, nt=False):
    dot = _dot_nt if nt else _dot
    a_hi, a_lo = _split2(a)
    b_hi, b_lo = _split2(b)
    return dot(a_hi, b_hi) + dot(a_hi, b_lo) + dot(a_lo, b_hi)


def _mm(a, b, precise):
    return _dot3(a, b) if precise else _dot(a.astype(BF16), b)


def _mm_parts(a, w_parts):
    if len(w_parts) == 1:
        return _dot(a.astype(BF16), w_parts[0])
    a_hi, a_lo = _split2(a)
    return _dot(a_hi, w_parts[0]) + _dot(a_hi, w_parts[1]) + _dot(a_lo, w_parts[0])


def _split3(x):
    hi = x.astype(BF16)
    r1 = x - hi.astype(F32)
    mid = r1.astype(BF16)
    lo = (r1 - mid.astype(F32)).astype(BF16)
    return hi, mid, lo


def _masked_softmax(s, mask):
    s = jnp.where(mask, s, NEG_BIG)
    e = jnp.where(mask, jnp.exp(s - jnp.max(s, axis=-1, keepdims=True)), 0.0)
    return e / jnp.maximum(jnp.sum(e, axis=-1, keepdims=True), 1e-30)


def _topk_mask(score, k):
    lane = lax.broadcasted_iota(jnp.int32, score.shape, 1)
    width = score.shape[1]
    sel = jnp.zeros(score.shape, F32)
    cur = score
    for _ in range(k):
        m = jnp.max(cur, axis=-1, keepdims=True)
        idx = jnp.min(jnp.where(cur == m, lane, width), axis=-1, keepdims=True)
        hit = lane == idx
        sel = jnp.where(hit, 1.0, sel)
        cur = jnp.where(hit, -jnp.inf, cur)
    return sel


def _topk_mask_t(score, k):
    sub = lax.broadcasted_iota(jnp.int32, score.shape, 0)
    height = score.shape[0]
    sel = jnp.zeros(score.shape, F32)
    cur = score
    for _ in range(k):
        m = jnp.max(cur, axis=0, keepdims=True)
        idx = jnp.min(jnp.where(cur == m, sub, height), axis=0, keepdims=True)
        hit = sub == idx
        sel = jnp.where(hit, 1.0, sel)
        cur = jnp.where(hit, -jnp.inf, cur)
    return sel


def _pool_prompt_body(x_ref, xp_ref, g_ref, w_ref, sc_ref, o_ref, tail_ref, ext_ref, *, tt):
    i = pl.program_id(1)
    x = x_ref[0]
    g = g_ref[...]
    u = _rms(x, g)
    up = _rms(xp_ref[0], g) * (i > 0).astype(F32)
    ext_ref[0:16] = up
    ext_ref[16:16 + tt] = u
    pos = i * tt + lax.broadcasted_iota(jnp.int32, (tt, 1), 0)
    for gi, w in enumerate(POOL_WINDOWS):
        ch = slice(gi * POOL_CH, (gi + 1) * POOL_CH)
        s = ext_ref[16:16 + tt, ch]
        for k in range(1, w):
            s = s + ext_ref[16 - k:16 - k + tt, ch]
        cnt = jnp.minimum(pos + 1, w).astype(F32)
        pooled = s / cnt - u[:, ch]
        mixed = _dot(pooled.astype(BF16), w_ref[gi])
        o_ref[0, :, ch] = x[:, ch] + mixed * sc_ref[:, ch]
    tail_ref[0] = ext_ref[tt:tt + 16]


def _pool_prompt(x, g, w_pool, scale, tt=512):
    b, t, d = x.shape
    per16 = tt // 16
    return pl.pallas_call(
        functools.partial(_pool_prompt_body, tt=tt),
        grid=(b, t // tt),
        in_specs=[
            pl.BlockSpec((1, tt, d), lambda bi, i: (bi, i, 0)),
            pl.BlockSpec((1, 16, d), lambda bi, i: (bi, jnp.maximum(i * per16 - 1, 0), 0)),
            pl.BlockSpec((1, d), lambda bi, i: (0, 0)),
            pl.BlockSpec((len(POOL_WINDOWS), POOL_CH, POOL_CH), lambda bi, i: (0, 0, 0)),
            pl.BlockSpec((1, d), lambda bi, i: (0, 0)),
        ],
        out_specs=[
            pl.BlockSpec((1, tt, d), lambda bi, i: (bi, i, 0)),
            pl.BlockSpec((1, 16, d), lambda bi, i: (bi, 0, 0)),
        ],
        out_shape=[jax.ShapeDtypeStruct((b, t, d), F32), jax.ShapeDtypeStruct((b, 16, d), F32)],
        scratch_shapes=[pltpu.VMEM((tt + 16, d), F32)],
        compiler_params=_params(("parallel", "arbitrary")),
        name="pool_prompt",
    )(x, x, g, w_pool, scale)


def _pool_sample_body(x_ref, st_ref, g_ref, w_ref, sc_ref, o_ref, nst_ref, ext_ref, *, t):
    u = _rms(x_ref[0], g_ref[...])
    ext_ref[...] = jnp.zeros(ext_ref.shape, F32)
    ext_ref[0:POOL_BUF] = st_ref[0]
    ext_ref[POOL_BUF:POOL_BUF + t] = u
    rows = 16
    for gi, w in enumerate(POOL_WINDOWS):
        ch = slice(gi * POOL_CH, (gi + 1) * POOL_CH)
        s = ext_ref[POOL_BUF:POOL_BUF + rows, ch]
        for k in range(1, w):
            s = s + ext_ref[POOL_BUF - k:POOL_BUF - k + rows, ch]
        pooled = s / float(w) - ext_ref[POOL_BUF:POOL_BUF + rows, ch]
        mixed = _dot3(pooled, w_ref[gi])
        o_ref[0, :, ch] = x_ref[0, :, ch] + mixed[0:t] * sc_ref[:, ch]
    nst_ref[0] = ext_ref[t:t + POOL_BUF]


def _pool_sample(x, state, g, w_pool, scale):
    b, t, d = x.shape
    return pl.pallas_call(
        functools.partial(_pool_sample_body, t=t),
        grid=(b,),
        in_specs=[
            pl.BlockSpec((1, t, d), lambda bi: (bi, 0, 0)),
            pl.BlockSpec((1, POOL_BUF, d), lambda bi: (bi, 0, 0)),
            pl.BlockSpec((1, d), lambda bi: (0, 0)),
            pl.BlockSpec((len(POOL_WINDOWS), POOL_CH, POOL_CH), lambda bi: (0, 0, 0)),
            pl.BlockSpec((1, d), lambda bi: (0, 0)),
        ],
        out_specs=[
            pl.BlockSpec((1, t, d), lambda bi: (bi, 0, 0)),
            pl.BlockSpec((1, POOL_BUF, d), lambda bi: (bi, 0, 0)),
        ],
        out_shape=[jax.ShapeDtypeStruct((b, t, d), F32), jax.ShapeDtypeStruct((b, POOL_BUF, d), F32)],
        scratch_shapes=[pltpu.VMEM((POOL_BUF + 16 + 1, d), F32)],
        compiler_params=_params(("parallel",)),
        name="pool_sample",
    )(x, state, g, w_pool, scale)


def _ffn_body(te_ref, x_ref, g_ref, wa_ref, wb_ref, wo_ref, o_ref, u_ref, acc_ref, *, norm, residual, precise):
    j = pl.program_id(1)

    @pl.when(j == 0)
    def _():
        x = x_ref[...].astype(F32)
        u = _rms(x, g_ref[...]) if norm else x
        u_ref[...] = u.astype(u_ref.dtype)
        acc_ref[...] = jnp.zeros(acc_ref.shape, F32)

    @pl.when(pl.program_id(0) < te_ref[pl.num_programs(0)])
    def _():
        u = u_ref[...]
        a = _mm(u, wa_ref[0], precise)
        b = _mm(u, wb_ref[0], precise)
        acc_ref[...] += _mm(a * jax.nn.sigmoid(a) * b, wo_ref[0], precise)

    @pl.when(j == pl.num_programs(1) - 1)
    def _():
        y = acc_ref[...]
        if residual:
            y = y + x_ref[...].astype(F32)
        o_ref[...] = y


def _ffn(x, g, w_in, w_out, tile_expert, *, norm, residual, tm, tf=1408, precise=False):
    n, d = x.shape
    nf = D_FF // tf
    grid_spec = pltpu.PrefetchScalarGridSpec(
        num_scalar_prefetch=1,
        grid=(n // tm, nf),
        in_specs=[
            pl.BlockSpec((tm, d), lambda i, j, te: (i, 0)),
            pl.BlockSpec((1, d), lambda i, j, te: (0, 0)),
            pl.BlockSpec((1, d, tf), lambda i, j, te: (te[i], 0, j)),
            pl.BlockSpec((1, d, tf), lambda i, j, te: (te[i], 0, nf + j)),
            pl.BlockSpec((1, tf, d), lambda i, j, te: (te[i], j, 0)),
        ],
        out_specs=pl.BlockSpec((tm, d), lambda i, j, te: (i, 0)),
        scratch_shapes=[pltpu.VMEM((tm, d), F32 if precise else BF16), pltpu.VMEM((tm, d), F32)],
    )
    return pl.pallas_call(
        functools.partial(_ffn_body, norm=norm, residual=residual, precise=precise),
        grid_spec=grid_spec,
        out_shape=jax.ShapeDtypeStruct((n, d), F32),
        compiler_params=_params(("parallel", "arbitrary")),
        name="swiglu",
    )(tile_expert, x, g, w_in, w_in, w_out)


def _proj_body(x_ref, gkv_ref, gq_ref, wkv_ref, wq_ref, wg_ref, kv_ref, q_ref, gate_ref, *extra, precise):
    x = x_ref[...]
    kv = _mm(_rms(x, gkv_ref[...]), wkv_ref[...], precise)
    kv_ref[...] = kv
    uq = _rms(x, gq_ref[...])
    q_ref[...] = (_mm(uq, wq_ref[...], precise) * (HEAD_DIM ** -0.5)).astype(q_ref.dtype)
    gate_ref[...] = jax.nn.sigmoid(_mm(uq, wg_ref[...], precise))
    if extra:
        kvb_ref, vt_ref = extra
        kvb_ref[...] = kv.astype(BF16)
        for c in range(vt_ref.shape[0]):
            rows = slice(c * KEY_CHUNK, (c + 1) * KEY_CHUNK)
            for br in range(2):
                v0 = (br + 1) * BRANCH_WIDTH + KV_HALF
                vt_ref[c, br * KV_HALF:(br + 1) * KV_HALF, :] = kv[rows, v0:v0 + KV_HALF].T.astype(BF16)


def _proj(x, g_kv, g_q, w_kv, w_q, w_g, tm, precise=False):
    n, d = x.shape
    full = lambda shape: pl.BlockSpec(shape, lambda i: (0,) * len(shape))
    tile = lambda w: pl.BlockSpec((tm, w), lambda i: (i, 0))
    out_specs = [tile(KV_WIDTH), tile(d), tile(2 * LANES)]
    out_shape = [jax.ShapeDtypeStruct((n, KV_WIDTH), F32), jax.ShapeDtypeStruct((n, d), F32 if precise else BF16),
                 jax.ShapeDtypeStruct((n, 2 * LANES), F32)]
    if not precise:
        chunks = tm // KEY_CHUNK
        out_specs += [tile(KV_WIDTH), pl.BlockSpec((chunks, 2 * KV_HALF, KEY_CHUNK), lambda i: (i, 0, 0))]
        out_shape += [jax.ShapeDtypeStruct((n, KV_WIDTH), BF16),
                      jax.ShapeDtypeStruct((n // KEY_CHUNK, 2 * KV_HALF, KEY_CHUNK), BF16)]
    return pl.pallas_call(
        functools.partial(_proj_body, precise=precise),
        grid=(n // tm,),
        in_specs=[tile(d), full((1, d)), full((1, d)), full(w_kv.shape), full(w_q.shape), full(w_g.shape)],
        out_specs=out_specs,
        out_shape=out_shape,
        compiler_params=_params(("parallel",)),
        name="kv_q_proj",
    )(x, g_kv, g_q, w_kv, w_q, w_g)


def _compress_body(*refs, n_src, rows_per_src, paged):
    src = refs[1:1 + n_src]
    w1_ref, pe_ref, w2_ref, o_ref, carry_ref = refs[1 + n_src:1 + n_src + 5]
    j = pl.program_id(2)
    n = n_src * rows_per_src // CMP_STRIDE

    @pl.when(j == 0)
    def _():
        carry_ref[...] = jnp.zeros(carry_ref.shape, F32)

    if paged:
        rows_ref = refs[-1]
        for pi, r in enumerate(src):
            rows_ref[pi * rows_per_src:(pi + 1) * rows_per_src, :] = r[0, 0].reshape(LANES, rows_per_src).T
    else:
        rows_ref = src[0].at[0]
    w1 = [w1_ref[0, i] for i in range(w1_ref.shape[1])]
    pe_out = _mm_parts(pe_ref[0], w1)
    pe_term = pe_out[0:1, 0:CMP_HIDDEN] + pe_out[1:2, CMP_HIDDEN:2 * CMP_HIDDEN]
    row = lax.broadcasted_iota(jnp.int32, (n, 1), 0)
    lhs = jnp.concatenate([rows_ref[pl.ds(s, n, stride=CMP_STRIDE), :] for s in range(CMP_STRIDE)],
                          axis=1)
    p = _mm_parts(lhs, w1)
    for g2 in range(2):
        pa = p[:, g2 * 2 * CMP_HIDDEN:g2 * 2 * CMP_HIDDEN + CMP_HIDDEN]
        pb = p[:, g2 * 2 * CMP_HIDDEN + CMP_HIDDEN:(g2 + 1) * 2 * CMP_HIDDEN]
        pa_prev = jnp.where(row == 0, carry_ref[g2:g2 + 1], pltpu.roll(pa, 1, 0))
        carry_ref[g2:g2 + 1] = pa[n - 1:n]
        hid = jax.nn.gelu(pa_prev + pb + pe_term)
        o_ref[0, 0, g2] = _mm(hid, w2_ref[0], paged)


def _compress(src_arrays, src_specs, grid, table, w1, pe, w2, rows_per_src, n_rows_total, paged):
    n_src = len(src_specs)
    n = n_src * rows_per_src // CMP_STRIDE
    b = grid[0]
    scratch = [pltpu.VMEM((8, CMP_HIDDEN), F32)]
    if paged:
        scratch.append(pltpu.VMEM((n_src * rows_per_src, LANES), F32))
    grid_spec = pltpu.PrefetchScalarGridSpec(
        num_scalar_prefetch=1,
        grid=grid,
        in_specs=list(src_specs) + [
            pl.BlockSpec((1,) + w1.shape[1:], lambda bi, c, j, tb: (c // 2, 0, 0, 0)),
            pl.BlockSpec((1,) + pe.shape[1:], lambda bi, c, j, tb: (c // 2, 0, 0)),
            pl.BlockSpec((1,) + w2.shape[1:], lambda bi, c, j, tb: (c // 2, 0, 0)),
        ],
        out_specs=pl.BlockSpec((1, 1, 2, n, HEAD_DIM), lambda bi, c, j, tb: (bi, c // 2, c % 2, j, 0)),
        scratch_shapes=scratch,
    )
    return pl.pallas_call(
        functools.partial(_compress_body, n_src=n_src, rows_per_src=rows_per_src, paged=paged),
        grid_spec=grid_spec,
        out_shape=jax.ShapeDtypeStruct((b, 2, N_KV_HEADS, n_rows_total, HEAD_DIM), F32),
        compiler_params=_params(("parallel", "parallel", "arbitrary")),
        name="compress",
    )(table, *src_arrays, w1, pe, w2)


def _flash_chunks(q_st, k_ref, v_ref, lane0, c_lo, c_hi, mask_fn, rows):
    def body(c, carry):
        m, l, acc = carry
        k = k_ref[0, pl.ds(pl.multiple_of(c * LANES, LANES), LANES), lane0:lane0 + HEAD_DIM]
        v = v_ref[0, pl.ds(pl.multiple_of(c * LANES, LANES), LANES), lane0:lane0 + HEAD_DIM]
        mask = mask_fn(c)
        s = jnp.where(mask, _dot_nt(q_st, k), NEG_BIG)
        m_new = jnp.maximum(m, jnp.max(s, axis=-1, keepdims=True))
        alpha = jnp.exp(m - m_new)
        p = jnp.where(mask, jnp.exp(s - m_new), 0.0)
        l = alpha * l + jnp.sum(p, axis=-1, keepdims=True)
        acc = alpha * acc + _dot(p.astype(BF16), v)
        return m_new, l, acc

    init = (jnp.full((rows, 1), NEG_BIG, F32), jnp.zeros((rows, 1), F32), jnp.zeros((rows, HEAD_DIM), F32))
    m, l, acc = lax.fori_loop(c_lo, c_hi, body, init)
    return acc / l


def _attn_prompt_body(q_ref, gate_ref, kvc_ref, ks_ref, vs_ref, kw_ref, vw_ref, msel_ref, o_ref, *, tq, n_cmp_rows, n_slc):
    i = pl.program_id(2)
    q0 = i * tq
    rows = HEADS_PER_KV * tq
    tok = lax.broadcasted_iota(jnp.int32, (tq, 1), 0)
    qpos_t = q0 + tok
    qpos = jnp.concatenate([qpos_t] * HEADS_PER_KV, axis=0)
    key_lane = lax.broadcasted_iota(jnp.int32, (1, LANES), 1)
    cmp_row = lax.broadcasted_iota(jnp.int32, (1, n_cmp_rows), 1)
    cmp_vis = (cmp_row >= 1) & (CMP_STRIDE * cmp_row + CMP_STRIDE - 1 <= qpos)
    blk = lax.broadcasted_iota(jnp.int32, (1, LANES), 1)
    e_row = lax.broadcasted_iota(jnp.int32, (LANES, LANES), 0)
    e_col = lax.broadcasted_iota(jnp.int32, (LANES, LANES), 1)
    q_all = q_ref[0]
    gates = gate_ref[0]
    outs = []
    for g2 in range(2):
        q_st = jnp.concatenate(
            [q_all[:, (g2 * HEADS_PER_KV + hh) * HEAD_DIM:(g2 * HEADS_PER_KV + hh + 1) * HEAD_DIM]
             for hh in range(HEADS_PER_KV)], axis=0)
        kc = kvc_ref[0, 0, g2].astype(BF16)
        vc = kvc_ref[0, 1, g2].astype(BF16)
        p_cmp = _masked_softmax(_dot_nt(q_st, kc), cmp_vis)
        o_cmp = _dot(p_cmp.astype(BF16), vc)
        p_sum = p_cmp[0:tq]
        for hh in range(1, HEADS_PER_KV):
            p_sum = p_sum + p_cmp[hh * tq:(hh + 1) * tq]
        hi, mid, lo = _split3(p_sum)
        msel = msel_ref[...]
        imp = _dot(hi, msel) + _dot(mid, msel) + _dot(lo, msel)
        local = blk == jnp.right_shift(qpos_t, 6)
        visible = blk * SEL_BLOCK <= qpos_t
        score = jnp.where(local, FORCE_LOCAL, jnp.where(blk == 0, FORCE_INITIAL, jnp.where(visible, imp, HIDE_BLOCK)))
        score = jnp.where(blk < n_slc, score, -jnp.inf)
        sel = _topk_mask(score, min(N_SEL, n_slc)).astype(BF16)

        def slc_mask(c):
            expand = (e_row == 2 * c + (e_col >= SEL_BLOCK).astype(jnp.int32)).astype(BF16)
            chosen = _dot(sel, expand)
            ok = jnp.where(c * LANES + key_lane <= qpos_t, chosen, 0.0)
            return jnp.concatenate([ok] * HEADS_PER_KV, axis=0) > 0.5

        def win_mask(c):
            kpos = c * LANES + key_lane
            ok = ((kpos <= qpos_t) & (kpos > qpos_t - WINDOW)).astype(F32)
            return jnp.concatenate([ok] * HEADS_PER_KV, axis=0) > 0.5

        lane0 = g2 * HEAD_DIM
        c_hi = (q0 + tq + LANES - 1) // LANES
        o_slc = _flash_chunks(q_st, ks_ref, vs_ref, lane0, 0, c_hi, slc_mask, rows)
        c_lo = jnp.maximum(q0 - WINDOW + 1, 0) // LANES
        o_win = _flash_chunks(q_st, kw_ref, vw_ref, lane0, c_lo, c_hi, win_mask, rows)
        for hh in range(HEADS_PER_KV):
            col = (g2 * HEADS_PER_KV + hh) * N_BRANCH
            rs = slice(hh * tq, (hh + 1) * tq)
            outs.append(gates[:, col:col + 1] * o_cmp[rs] + gates[:, col + 1:col + 2] * o_slc[rs]
                        + gates[:, col + 2:col + 3] * o_win[rs])
    o_ref[0] = jnp.concatenate(outs, axis=1)


KEY_CHUNK = 256


def _flash_keys(q_st, k_ref, v_ref, lane0, c_lo, c_hi, bias_fn, m_ref, l_ref, acc_ref, tq):
    rows = q_st.shape[0]
    m_ref[...] = jnp.full(m_ref.shape, NEG_BIG, F32)
    l_ref[...] = jnp.zeros(l_ref.shape, F32)
    acc_ref[...] = jnp.zeros(acc_ref.shape, F32)

    def body(c, carry):
        start = pl.multiple_of(c * KEY_CHUNK, KEY_CHUNK)
        k = k_ref[0, pl.ds(start, KEY_CHUNK), lane0:lane0 + HEAD_DIM]
        v = v_ref[0, pl.ds(start, KEY_CHUNK), lane0:lane0 + HEAD_DIM]
        s = _dot_nt(q_st, k).reshape(HEADS_PER_KV, tq, KEY_CHUNK) + bias_fn(c)[None]
        s = s.reshape(rows, KEY_CHUNK)
        m_old = m_ref[...]
        m_new = jnp.maximum(m_old, jnp.max(s, axis=-1, keepdims=True))
        alpha = jnp.exp(m_old - m_new)
        p = jnp.exp(s - m_new)
        l_ref[...] = alpha * l_ref[...] + jnp.sum(p, axis=-1, keepdims=True)
        acc_ref[...] = alpha * acc_ref[...] + _dot(p.astype(BF16), v)
        m_ref[...] = m_new
        return carry

    lax.fori_loop(c_lo, c_hi, body, 0)
    return acc_ref[...] / l_ref[...]


def _attn_prompt_body2(q_ref, gate_ref, kvc_ref, ks_ref, vs_ref, kw_ref, vw_ref, msel_ref, o_ref,
                       bias_ref, m_ref, l_ref, acc_ref, *, tq, n_cmp_rows, n_slc):
    i = pl.program_id(2)
    q0 = i * tq
    rows = HEADS_PER_KV * tq
    qpos_t = q0 + lax.broadcasted_iota(jnp.int32, (tq, 1), 0)
    qpos = jnp.concatenate([qpos_t] * HEADS_PER_KV, axis=0)
    key_lane = lax.broadcasted_iota(jnp.int32, (1, KEY_CHUNK), 1)
    cmp_row = lax.broadcasted_iota(jnp.int32, (1, n_cmp_rows), 1)
    cmp_vis = (cmp_row >= 1) & (CMP_STRIDE * cmp_row + CMP_STRIDE - 1 <= qpos)
    blk = lax.broadcasted_iota(jnp.int32, (1, LANES), 1)
    e_row = lax.broadcasted_iota(jnp.int32, (LANES, KEY_CHUNK), 0)
    e_blk = jnp.right_shift(lax.broadcasted_iota(jnp.int32, (LANES, KEY_CHUNK), 1), 6)
    blocks_per_chunk = KEY_CHUNK // SEL_BLOCK
    q_all = q_ref[0]
    gates = gate_ref[0]
    msel = msel_ref[...]
    local = blk == jnp.right_shift(qpos_t, 6)
    visible = blk * SEL_BLOCK <= qpos_t
    c_hi = (q0 + tq + KEY_CHUNK - 1) // KEY_CHUNK
    c_lo = jnp.maximum(q0 - WINDOW + 1, 0) // KEY_CHUNK

    q_sts, o_cmps, scores = [], [], []
    for g2 in range(2):
        q_st = jnp.concatenate(
            [q_all[:, (g2 * HEADS_PER_KV + hh) * HEAD_DIM:(g2 * HEADS_PER_KV + hh + 1) * HEAD_DIM]
             for hh in range(HEADS_PER_KV)], axis=0)
        kc = kvc_ref[0, 0, g2].astype(BF16)
        vc = kvc_ref[0, 1, g2].astype(BF16)
        p_cmp = _masked_softmax(_dot_nt(q_st, kc), cmp_vis)
        o_cmps.append(_dot(p_cmp.astype(BF16), vc))
        p_sum = p_cmp[0:tq]
        for hh in range(1, HEADS_PER_KV):
            p_sum = p_sum + p_cmp[hh * tq:(hh + 1) * tq]
        hi, mid, lo = _split3(p_sum)
        imp = _dot(hi, msel) + _dot(mid, msel) + _dot(lo, msel)
        score = jnp.where(local, FORCE_LOCAL, jnp.where(blk == 0, FORCE_INITIAL, jnp.where(visible, imp, HIDE_BLOCK)))
        scores.append(jnp.where(blk < n_slc, score, -jnp.inf))
        q_sts.append(q_st)
    sel_both = _topk_mask(jnp.concatenate(scores, axis=0), min(N_SEL, n_slc)).astype(BF16)

    def win_bias(c):
        kpos = c * KEY_CHUNK + key_lane
        return jnp.where((kpos <= qpos_t) & (kpos > qpos_t - WINDOW), 0.0, NEG_BIG)

    outs = []
    for g2 in range(2):
        sel = sel_both[g2 * tq:(g2 + 1) * tq]

        def build_bias(c, carry):
            expand = (e_row == blocks_per_chunk * c + e_blk).astype(BF16)
            chosen = _dot(sel, expand)
            ok = (chosen > 0.5) & (c * KEY_CHUNK + key_lane <= qpos_t)
            bias_ref[c] = jnp.where(ok, 0.0, NEG_BIG)
            return carry

        lax.fori_loop(0, c_hi, build_bias, 0)
        lane0 = g2 * HEAD_DIM
        o_slc = _flash_keys(q_sts[g2], ks_ref, vs_ref, lane0, 0, c_hi, lambda c: bias_ref[c], m_ref, l_ref, acc_ref, tq)
        o_win = _flash_keys(q_sts[g2], kw_ref, vw_ref, lane0, c_lo, c_hi, win_bias, m_ref, l_ref, acc_ref, tq)
        for hh in range(HEADS_PER_KV):
            col = (g2 * HEADS_PER_KV + hh) * N_BRANCH
            rs = slice(hh * tq, (hh + 1) * tq)
            outs.append(gates[:, col:col + 1] * o_cmps[g2][rs] + gates[:, col + 1:col + 2] * o_slc[rs]
                        + gates[:, col + 2:col + 3] * o_win[rs])
    o_ref[0] = jnp.concatenate(outs, axis=1)


ONES_ROWS = 16


def _flash_keys_t(k_ref, vt_ref, row0, q_t, c_lo, c_hi, bias_fn, m_ref, acc_ref):
    m_ref[...] = jnp.full(m_ref.shape, NEG_BIG, F32)
    acc_ref[...] = jnp.zeros(acc_ref.shape, F32)
    ones = jnp.ones((ONES_ROWS, KEY_CHUNK), BF16)

    def body(c, carry):
        start = pl.multiple_of(c * KEY_CHUNK, KEY_CHUNK)
        k = k_ref[0, pl.ds(start, KEY_CHUNK), :]
        v_aug = jnp.concatenate([vt_ref[0, c, row0:row0 + HEAD_DIM, :], ones], axis=0)
        bias = bias_fn(c)
        s = _dot(k, q_all) + jnp.concatenate([bias] * HEADS_PER_KV, axis=1)
        m_old = m_ref[...]
        m_new = jnp.maximum(m_old, jnp.max(s, axis=0, keepdims=True))
        alpha = jnp.exp(m_old - m_new)
        p = jnp.exp(s - m_new).astype(BF16)
        acc_ref[...] = alpha * acc_ref[...] + _dot(v_aug, p)
        m_ref[...] = m_new
        return carry

    q_all = jnp.concatenate(q_t, axis=1)
    tq = q_t[0].shape[1]
    lax.fori_loop(c_lo, c_hi, body, 0)
    return [acc_ref[:, hh * tq:(hh + 1) * tq] for hh in range(HEADS_PER_KV)]


def _attn_prompt_body3(q_ref, gate_ref, kvc_ref, ks_ref, vst_ref, kw_ref, vwt_ref, msel_ref, o_ref,
                       bias_ref, m_ref, acc_ref, *, tq, n_cmp_rows, n_slc):
    i = pl.program_id(2)
    q0 = i * tq
    qpos_t = q0 + lax.broadcasted_iota(jnp.int32, (tq, 1), 0)
    qpos = jnp.concatenate([qpos_t] * HEADS_PER_KV, axis=0)
    cmp_row = lax.broadcasted_iota(jnp.int32, (1, n_cmp_rows), 1)
    cmp_vis = (cmp_row >= 1) & (CMP_STRIDE * cmp_row + CMP_STRIDE - 1 <= qpos)
    q_all = q_ref[0]
    gates = gate_ref[0]
    msel_t = msel_ref[...]
    nb = msel_t.shape[0]
    qpos_l = q0 + lax.broadcasted_iota(jnp.int32, (1, tq), 1)
    blk = lax.broadcasted_iota(jnp.int32, (nb, 1), 0)
    local = blk == jnp.right_shift(qpos_l, 6)
    visible = blk * SEL_BLOCK <= qpos_l
    c_hi = (q0 + tq + KEY_CHUNK - 1) // KEY_CHUNK
    c_lo = jnp.maximum(q0 - WINDOW + 1, 0) // KEY_CHUNK

    o_cmps, scores = [], []
    for g2 in range(2):
        q_st = jnp.concatenate(
            [q_all[:, (g2 * HEADS_PER_KV + hh) * HEAD_DIM:(g2 * HEADS_PER_KV + hh + 1) * HEAD_DIM]
             for hh in range(HEADS_PER_KV)], axis=0)
        kc = kvc_ref[0, 0, g2].astype(BF16)
        vc = kvc_ref[0, 1, g2].astype(BF16)
        p_cmp = _masked_softmax(_dot_nt(q_st, kc), cmp_vis)
        o_cmps.append(_dot(p_cmp.astype(BF16), vc))
        p_sum = p_cmp[0:tq]
        for hh in range(1, HEADS_PER_KV):
            p_sum = p_sum + p_cmp[hh * tq:(hh + 1) * tq]
        hi, mid, lo = _split3(p_sum)
        imp = _dot_nt(msel_t, hi) + _dot_nt(msel_t, mid) + _dot_nt(msel_t, lo)
        score = jnp.where(local, FORCE_LOCAL, jnp.where(blk == 0, FORCE_INITIAL, jnp.where(visible, imp, HIDE_BLOCK)))
        scores.append(jnp.where(blk < n_slc, score, -jnp.inf))
    sel_both = _topk_mask_t(jnp.concatenate(scores, axis=1), min(N_SEL, n_slc)).astype(BF16)

    key_sub = lax.broadcasted_iota(jnp.int32, (KEY_CHUNK, 1), 0)
    key_blk = jnp.right_shift(key_sub, 6)
    blk_l = lax.broadcasted_iota(jnp.int32, (1, nb), 1)
    blocks_per_chunk = KEY_CHUNK // SEL_BLOCK

    def win_bias(c):
        kpos = c * KEY_CHUNK + key_sub
        return jnp.where((kpos <= qpos_l) & (kpos > qpos_l - WINDOW), 0.0, NEG_BIG)

    outs = []
    for g2 in range(2):
        sel_t = sel_both[:, g2 * tq:(g2 + 1) * tq]

        def build_bias(c, carry):
            expand = (blocks_per_chunk * c + key_blk == blk_l).astype(BF16)
            ok = (_dot(expand, sel_t) > 0.5) & (c * KEY_CHUNK + key_sub <= qpos_l)
            bias_ref[c] = jnp.where(ok, 0.0, NEG_BIG)
            return carry

        lax.fori_loop(0, c_hi, build_bias, 0)
        q_g = q_all[:, g2 * KV_HALF:(g2 + 1) * KV_HALF].astype(F32).T.astype(BF16)
        zeros = jnp.zeros((HEAD_DIM, tq), BF16)
        q_t = []
        for hh in range(HEADS_PER_KV):
            q_h = q_g[hh * HEAD_DIM:(hh + 1) * HEAD_DIM]
            q_t.append(jnp.concatenate([q_h, zeros] if g2 == 0 else [zeros, q_h], axis=0))
        row0 = g2 * HEAD_DIM
        normalise = lambda a: (a[0:HEAD_DIM] / a[HEAD_DIM:HEAD_DIM + 1]).T
        o_slc = [normalise(a) for a in
                 _flash_keys_t(ks_ref, vst_ref, row0, q_t, 0, c_hi, lambda c: bias_ref[c], m_ref, acc_ref)]
        o_win = [normalise(a) for a in _flash_keys_t(kw_ref, vwt_ref, row0, q_t, c_lo, c_hi, win_bias, m_ref, acc_ref)]
        for hh in range(HEADS_PER_KV):
            col = (g2 * HEADS_PER_KV + hh) * N_BRANCH
            outs.append(gates[:, col:col + 1] * o_cmps[g2][hh * tq:(hh + 1) * tq] + gates[:, col + 1:col + 2] * o_slc[hh]
                        + gates[:, col + 2:col + 3] * o_win[hh])
    o_ref[0] = jnp.concatenate(outs, axis=1)


def _attn_prompt(q, gates, kvc, kvb, vt, msel, tq=256):
    b, t, d = q.shape
    r = kvc.shape[3]
    half = 2 * HEAD_DIM
    n_chunks = t // KEY_CHUNK
    k_spec = lambda base: pl.BlockSpec((1, t, half), lambda bi, gp, i: (bi, 0, base + gp))
    vt_spec = lambda base: pl.BlockSpec((1, n_chunks, half, KEY_CHUNK), lambda bi, gp, i: (bi, 0, base + gp, 0))
    slc0 = BRANCH_WIDTH // half
    win0 = 2 * BRANCH_WIDTH // half
    return pl.pallas_call(
        functools.partial(_attn_prompt_body3, tq=tq, n_cmp_rows=r, n_slc=t // SEL_BLOCK),
        grid=(b, 2, t // tq),
        scratch_shapes=[pltpu.VMEM((n_chunks, KEY_CHUNK, tq), F32),
                        pltpu.VMEM((1, HEADS_PER_KV * tq), F32),
                        pltpu.VMEM((HEAD_DIM + ONES_ROWS, HEADS_PER_KV * tq), F32)],
        in_specs=[
            pl.BlockSpec((1, tq, 2 * HEADS_PER_KV * HEAD_DIM), lambda bi, gp, i: (bi, i, gp)),
            pl.BlockSpec((1, tq, LANES), lambda bi, gp, i: (bi, i, gp)),
            pl.BlockSpec((1, 2, 2, r, HEAD_DIM), lambda bi, gp, i: (bi, 0, gp, 0, 0)),
            k_spec(slc0), vt_spec(0), k_spec(win0), vt_spec(KV_HALF // half),
            pl.BlockSpec(msel.shape, lambda bi, gp, i: (0, 0)),
        ],
        out_specs=pl.BlockSpec((1, tq, 2 * HEADS_PER_KV * HEAD_DIM), lambda bi, gp, i: (bi, i, gp)),
        out_shape=jax.ShapeDtypeStruct((b, t, d), F32),
        compiler_params=_params(("parallel", "parallel", "arbitrary")),
        name="attn_prompt",
    )(q, gates, kvc, kvb, vt, kvb, vt, msel)


T_PAD = 8
S_ROWS = N_KV_HEADS * HEADS_PER_KV * T_PAD


def _attn_sample_cmp_body(q_ref, kvc_ref, msel_ref, o_ref, sel_ref, *, past_len, n_cmp_rows, n_slc):
    tok = lax.broadcasted_iota(jnp.int32, (T_PAD, 1), 0)
    qpos_t = past_len + tok
    qpos = jnp.concatenate([qpos_t] * HEADS_PER_KV, axis=0)
    cmp_row = lax.broadcasted_iota(jnp.int32, (1, n_cmp_rows), 1)
    cmp_vis = (cmp_row >= 1) & (CMP_STRIDE * cmp_row + CMP_STRIDE - 1 <= qpos)
    blk = lax.broadcasted_iota(jnp.int32, (1, msel_ref.shape[1]), 1)
    msel = msel_ref[...]
    scores = []
    for g in range(N_KV_HEADS):
        q_st = q_ref[0, g]
        p_cmp = _masked_softmax(_dot3(q_st, kvc_ref[0, 0, g], nt=True), cmp_vis)
        o_ref[0, g] = _dot3(p_cmp, kvc_ref[0, 1, g])
        p_sum = p_cmp[0:T_PAD]
        for hh in range(1, HEADS_PER_KV):
            p_sum = p_sum + p_cmp[hh * T_PAD:(hh + 1) * T_PAD]
        hi, mid, lo = _split3(p_sum)
        imp = _dot(hi, msel) + _dot(mid, msel) + _dot(lo, msel)
        local = blk == jnp.right_shift(qpos_t, 6)
        visible = blk * SEL_BLOCK <= qpos_t
        score = jnp.where(local, FORCE_LOCAL, jnp.where(blk == 0, FORCE_INITIAL, jnp.where(visible, imp, HIDE_BLOCK)))
        scores.append(jnp.where(blk < n_slc, score, -jnp.inf))
    sel = _topk_mask(jnp.concatenate(scores, axis=0), min(N_SEL, n_slc))
    for g in range(N_KV_HEADS):
        sel_ref[0, g] = sel[g * T_PAD:(g + 1) * T_PAD]


def _attn_sample_cmp(q32, kvc, msel, past_len, n_slc):
    b = q32.shape[0]
    r = kvc.shape[3]
    w = msel.shape[1]
    return pl.pallas_call(
        functools.partial(_attn_sample_cmp_body, past_len=past_len, n_cmp_rows=r, n_slc=n_slc),
        grid=(b,),
        in_specs=[
            pl.BlockSpec((1, N_KV_HEADS, HEADS_PER_KV * T_PAD, HEAD_DIM), lambda bi: (bi, 0, 0, 0)),
            pl.BlockSpec((1, 2, N_KV_HEADS, r, HEAD_DIM), lambda bi: (bi, 0, 0, 0, 0)),
            pl.BlockSpec(msel.shape, lambda bi: (0, 0)),
        ],
        out_specs=[
            pl.BlockSpec((1, N_KV_HEADS, HEADS_PER_KV * T_PAD, HEAD_DIM), lambda bi: (bi, 0, 0, 0)),
            pl.BlockSpec((1, N_KV_HEADS, T_PAD, w), lambda bi: (bi, 0, 0, 0)),
        ],
        out_shape=[jax.ShapeDtypeStruct((b, N_KV_HEADS, HEADS_PER_KV * T_PAD, HEAD_DIM), F32),
                   jax.ShapeDtypeStruct((b, N_KV_HEADS, T_PAD, w), F32)],
        compiler_params=_params(("parallel",)),
        name="attn_sample_cmp",
    )(q32, kvc, msel)


def _attn_sample_body(*refs, n_pages, t_new):
    pages = refs[1:1 + n_pages]
    (q_ref, sel_ref, new_ref, win_ref, ocmp_ref, gate_ref, o_ref, m_ref, l_ref, acc_ref) = refs[1 + n_pages:]
    j = pl.program_id(1)
    row = lax.broadcasted_iota(jnp.int32, (S_ROWS, 1), 0)
    tok = jnp.bitwise_and(row, T_PAD - 1)
    row_g = jnp.right_shift(row, 5)
    lane = lax.broadcasted_iota(jnp.int32, (1, LANES), 1)
    q = q_ref[0]

    @pl.when(j == 0)
    def _():
        m_ref[...] = jnp.full(m_ref.shape, NEG_BIG, F32)
        l_ref[...] = jnp.zeros(l_ref.shape, F32)
        acc_ref[...] = jnp.zeros(acc_ref.shape, F32)

    sel = sel_ref[0, 0]
    s_parts, m_parts = [], []
    for pi in range(n_pages):
        k_t = pages[pi][0, 0].reshape(KV_HALF, PAGE_SIZE)
        s_parts.append(_dot3(q, k_t))
        chosen = jnp.where(lane < SEL_BLOCK, sel[:, 2 * pi:2 * pi + 1], sel[:, 2 * pi + 1:2 * pi + 2])
        m_parts.append(chosen > 0.5)
    mask = jnp.concatenate(m_parts, axis=1)
    s = jnp.where(mask, jnp.concatenate(s_parts, axis=1), NEG_BIG)
    m_old = m_ref[...]
    m_new = jnp.maximum(m_old, jnp.max(s, axis=-1, keepdims=True))
    alpha = jnp.exp(m_old - m_new)
    p = jnp.where(mask, jnp.exp(s - m_new), 0.0)
    l_ref[...] = alpha * l_ref[...] + jnp.sum(p, axis=-1, keepdims=True)
    pv = jnp.zeros((S_ROWS, KV_HALF), F32)
    for pi in range(n_pages):
        v_t = pages[pi][0, 1].reshape(KV_HALF, PAGE_SIZE)
        pv = pv + _dot3(p[:, pi * LANES:(pi + 1) * LANES], v_t, nt=True)
    acc_ref[...] = alpha * acc_ref[...] + pv
    m_ref[...] = m_new

    @pl.when(j == pl.num_programs(1) - 1)
    def _():
        def own_head(x):
            out = jnp.zeros((S_ROWS, HEAD_DIM), F32)
            for g in range(N_KV_HEADS):
                out = out + jnp.where(row_g == g, x[:, g * HEAD_DIM:(g + 1) * HEAD_DIM], 0.0)
            return out

        new_ok = (lane <= tok) & (lane < t_new)
        k_new = new_ref[0, 0, :, 0:KV_HALF]
        v_new = new_ref[0, 0, :, KV_HALF:2 * KV_HALF]
        s_n = jnp.where(new_ok, _dot3(q, k_new, nt=True), NEG_BIG)
        m_old = m_ref[...]
        m_fin = jnp.maximum(m_old, jnp.max(s_n, axis=-1, keepdims=True))
        alpha = jnp.exp(m_old - m_fin)
        p_n = jnp.where(new_ok, jnp.exp(s_n - m_fin), 0.0)
        l_fin = alpha * l_ref[...] + jnp.sum(p_n, axis=-1, keepdims=True)
        o_slc = own_head((alpha * acc_ref[...] + _dot3(p_n, v_new)) / l_fin)
        nw = win_ref.shape[-1]
        kw_t = win_ref[0, 0].reshape(KV_HALF, nw)
        vw_t = win_ref[0, 1].reshape(KV_HALF, nw)
        kw_new = new_ref[0, 1, :, 0:KV_HALF]
        vw_new = new_ref[0, 1, :, KV_HALF:2 * KV_HALF]
        widx = lax.broadcasted_iota(jnp.int32, (1, nw), 1)
        w_mask = jnp.concatenate([jnp.broadcast_to(widx > tok, (S_ROWS, nw)),
                                  jnp.broadcast_to(new_ok, (S_ROWS, LANES))], axis=1)
        s_w = jnp.concatenate([_dot3(q, kw_t), _dot3(q, kw_new, nt=True)], axis=1)
        p_w = _masked_softmax(s_w, w_mask)
        o_win = own_head(_dot3(p_w[:, 0:nw], vw_t, nt=True) + _dot3(p_w[:, nw:], vw_new))
        gates = gate_ref[0]
        o_ref[0] = gates[:, 0:1] * ocmp_ref[0] + gates[:, 1:2] * o_slc + gates[:, 2:3] * o_win


def _attn_sample(page_table, cache_slc, q_st, sel_steps, kv_new, t_new, cache_win, o_cmp, gate_rows, n_pages_step):
    b, n_pages = page_table.shape
    steps = n_pages // n_pages_step
    page_spec = lambda pi: pl.BlockSpec(
        (1, 2, N_KV_HEADS, HEAD_DIM, PAGE_SIZE), lambda bi, j, pt: (pt[bi, j * n_pages_step + pi], 0, 0, 0, 0))
    grid_spec = pltpu.PrefetchScalarGridSpec(
        num_scalar_prefetch=1,
        grid=(b, steps),
        in_specs=[page_spec(pi) for pi in range(n_pages_step)] + [
            pl.BlockSpec((1, S_ROWS, KV_HALF), lambda bi, j, pt: (bi, 0, 0)),
            pl.BlockSpec((1, 1, S_ROWS, LANES), lambda bi, j, pt: (bi, j, 0, 0)),
            pl.BlockSpec((1, 2, LANES, BRANCH_WIDTH), lambda bi, j, pt: (bi, 0, 0, 0)),
            pl.BlockSpec((1,) + cache_win.shape[1:], lambda bi, j, pt: (bi, 0, 0, 0, 0)),
            pl.BlockSpec((1, S_ROWS, HEAD_DIM), lambda bi, j, pt: (bi, 0, 0)),
            pl.BlockSpec((1, S_ROWS, LANES), lambda bi, j, pt: (bi, 0, 0)),
        ],
        out_specs=pl.BlockSpec((1, S_ROWS, HEAD_DIM), lambda bi, j, pt: (bi, 0, 0)),
        scratch_shapes=[pltpu.VMEM((S_ROWS, 1), F32), pltpu.VMEM((S_ROWS, 1), F32), pltpu.VMEM((S_ROWS, KV_HALF), F32)],
    )
    return pl.pallas_call(
        functools.partial(_attn_sample_body, n_pages=n_pages_step, t_new=t_new),
        grid_spec=grid_spec,
        out_shape=jax.ShapeDtypeStruct((b, S_ROWS, HEAD_DIM), F32),
        compiler_params=_params(("parallel", "arbitrary")),
        name="attn_sample",
    )(page_table, *([cache_slc] * n_pages_step), q_st, sel_steps, kv_new, cache_win, o_cmp, gate_rows)


def _oproj_router_body(h_ref, a_ref, wo_ref, g_ref, wr_ref, br_ref, h3_ref, u_ref, route_ref, *, precise):
    h3 = h_ref[...] + _mm(a_ref[...], wo_ref[...], precise)
    h3_ref[...] = h3
    u = _rms(h3, g_ref[...])
    u_ref[...] = u
    logits = _dot3(u, wr_ref[...]) + br_ref[...]
    lane = lax.broadcasted_iota(jnp.int32, logits.shape, 1)
    logits = jnp.where(lane < N_EXPERTS, logits, -jnp.inf)
    v1 = jnp.max(logits, axis=-1, keepdims=True)
    i1 = jnp.min(jnp.where(logits == v1, lane, LANES), axis=-1, keepdims=True)
    rest = jnp.where(lane == i1, -jnp.inf, logits)
    v2 = jnp.max(rest, axis=-1, keepdims=True)
    i2 = jnp.min(jnp.where(rest == v2, lane, LANES), axis=-1, keepdims=True)
    e2 = jnp.exp(v2 - v1)
    w1 = 1.0 / (1.0 + e2)
    w2 = e2 / (1.0 + e2)
    route_ref[...] = jnp.where(lane == 0, i1.astype(F32), jnp.where(lane == 1, i2.astype(F32),
                               jnp.where(lane == 2, w1, jnp.where(lane == 3, w2, 0.0))))


def _oproj_router(h, attn, w_o, g, w_r, b_r, tm, precise=False):
    n, d = h.shape
    full = lambda shape: pl.BlockSpec(shape, lambda i: (0,) * len(shape))
    tile = lambda w: pl.BlockSpec((tm, w), lambda i: (i, 0))
    return pl.pallas_call(
        functools.partial(_oproj_router_body, precise=precise),
        grid=(n // tm,),
        in_specs=[tile(d), tile(d), full(w_o.shape), full((1, d)), full(w_r.shape), full(b_r.shape)],
        out_specs=[tile(d), tile(d), tile(LANES)],
        out_shape=[jax.ShapeDtypeStruct((n, d), F32), jax.ShapeDtypeStruct((n, d), F32),
                   jax.ShapeDtypeStruct((n, LANES), F32)],
        compiler_params=_params(("parallel",)),
        name="oproj_router",
    )(h, attn, w_o, g, w_r, b_r)


def _scatter_rows_body(dest_ref, u_ref, xs_in_ref, xs_ref, sem, *, td):
    del xs_in_ref

    def row_copy(r, k):
        return pltpu.make_async_copy(u_ref.at[pl.ds(r, 1)], xs_ref.at[pl.ds(dest_ref[2 * r + k], 1)], sem)

    def start(r, carry):
        row_copy(r, 0).start()
        row_copy(r, 1).start()
        return carry

    def wait(r, carry):
        row_copy(r, 0).wait()
        row_copy(r, 1).wait()
        return carry

    lax.fori_loop(0, td, start, 0)
    lax.fori_loop(0, td, wait, 0)


def _scatter_rows(u, dest_flat, xs, td):
    n, d = u.shape
    return pl.pallas_call(
        functools.partial(_scatter_rows_body, td=td),
        grid=(n // td,),
        in_specs=[pl.BlockSpec((2 * td,), lambda i: (i,), memory_space=pltpu.SMEM),
                  pl.BlockSpec((td, d), lambda i: (i, 0)),
                  pl.BlockSpec(memory_space=pl.ANY)],
        out_specs=pl.BlockSpec(memory_space=pl.ANY),
        out_shape=jax.ShapeDtypeStruct(xs.shape, xs.dtype),
        scratch_shapes=[pltpu.SemaphoreType.DMA(())],
        input_output_aliases={2: 0},
        compiler_params=_params(("arbitrary",)),
        name="moe_scatter_rows",
    )(dest_flat, u, xs)


def _combine_body(dest_ref, h_ref, route_ref, g_ref, y_ref, o_ref, ybuf_ref, sem, *, tc):
    def row_copy(r, k):
        return pltpu.make_async_copy(y_ref.at[pl.ds(dest_ref[2 * r + k], 1)], ybuf_ref.at[k, pl.ds(r, 1)], sem)

    def start(r, carry):
        row_copy(r, 0).start()
        row_copy(r, 1).start()
        return carry

    def wait(r, carry):
        row_copy(r, 0).wait()
        row_copy(r, 1).wait()
        return carry

    lax.fori_loop(0, tc, start, 0)
    lax.fori_loop(0, tc, wait, 0)
    r = route_ref[...]
    h4 = h_ref[...] + (r[:, 2:3] * ybuf_ref[0] + r[:, 3:4] * ybuf_ref[1])
    o_ref[...] = _rms(h4, g_ref[...])


def _combine(h, y_sorted, dest_flat, route, g, tc):
    n, d = h.shape
    tile = lambda w: pl.BlockSpec((tc, w), lambda i: (i, 0))
    return pl.pallas_call(
        functools.partial(_combine_body, tc=tc),
        grid=(n // tc,),
        in_specs=[pl.BlockSpec((2 * tc,), lambda i: (i,), memory_space=pltpu.SMEM),
                  tile(d), tile(LANES), pl.BlockSpec((1, d), lambda i: (0, 0)),
                  pl.BlockSpec(memory_space=pl.ANY)],
        out_specs=tile(d),
        out_shape=jax.ShapeDtypeStruct((n, d), F32),
        scratch_shapes=[pltpu.VMEM((2, tc, d), F32), pltpu.SemaphoreType.DMA(())],
        compiler_params=_params(("arbitrary",)),
        name="moe_combine_norm",
    )(dest_flat, h, route, g, y_sorted)


def _selection_rows(n_rows, width):
    r = np.arange(n_rows)[:, None]
    j = np.arange(width)[None, :]
    i = r - 1
    ov = (r >= 1) & (CMP_STRIDE * i < SEL_BLOCK * (j + 1)) & (CMP_STRIDE * i + CMP_BLOCK > SEL_BLOCK * j)
    return jnp.asarray(ov.astype(np.float32), dtype=BF16)


def _compress_weights(w_cmp1, pe_cmp):
    w = w_cmp1.reshape(2, 2, CMP_STRIDE, HEAD_DIM, CMP_HIDDEN).transpose(0, 2, 3, 1, 4)
    eye = jnp.eye(2, dtype=w.dtype)
    w1 = jnp.einsum('asdvh,gk->asgdkvh', w, eye).reshape(2, CMP_STRIDE * 2 * HEAD_DIM, 4 * CMP_HIDDEN)
    pe = pe_cmp.reshape(2, 2, CMP_STRIDE, 1, HEAD_DIM)
    pe = jnp.concatenate([pe, jnp.zeros_like(pe)], axis=3).reshape(2, 2, CMP_STRIDE * 2 * HEAD_DIM)
    pe = jnp.concatenate([pe, jnp.zeros((2, 6, pe.shape[-1]), pe.dtype)], axis=1)
    return w1, pe


def _split_bits(w):
    hi = lax.bitcast_convert_type(lax.bitcast_convert_type(w, jnp.uint32) & jnp.uint32(0xFFFF0000), F32)
    return hi.astype(BF16), (w - hi).astype(BF16)


def _gate_weights(w_qg_l):
    wg = w_qg_l[:, N_HEADS * HEAD_DIM:].reshape(D_MODEL, 2, 2 * HEADS_PER_KV * N_BRANCH)
    wg = jnp.pad(wg, ((0, 0), (0, 0), (0, LANES - wg.shape[-1])))
    return wg.reshape(D_MODEL, 2 * LANES)


def _dispatch(route, tm):
    n = route.shape[0]
    e_idx = route[:, 0:2].astype(jnp.int32)
    flat = e_idx.reshape(-1)
    onehot = (flat[:, None] == jnp.arange(N_EXPERTS)[None, :]).astype(jnp.int32)
    counts = jnp.sum(onehot, axis=0)
    padded = ((counts + tm - 1) // tm) * tm
    starts = jnp.cumsum(padded) - padded
    rank = jnp.take_along_axis(jnp.cumsum(onehot, axis=0), flat[:, None], axis=1)[:, 0] - 1
    dest = starts[flat] + rank
    n_tiles = -(-2 * n // tm) + N_EXPERTS
    tile_start = jnp.arange(n_tiles, dtype=jnp.int32) * tm
    ends = jnp.cumsum(padded)
    tile_expert = jnp.minimum(jnp.sum((tile_start[:, None] >= ends[None, :]).astype(jnp.int32), axis=1), N_EXPERTS - 1)
    tile_info = jnp.concatenate([tile_expert, ends[-1:] // tm]).astype(jnp.int32)
    return dest.astype(jnp.int32), tile_info, n_tiles * tm


def kernel(x_prompt, x_sample, cache_cmp, cache_slc, cache_win, state_pool, page_table, norm_mix, norm_ffn, norm_kv, norm_final, w_pool, pool_scale, w_kv, w_cmp1, pe_cmp, w_cmp2, w_qg, w_o, w_ffn_in, w_ffn_out, w_router, b_router, w_exp_in, w_exp_out):
    bp, tp, d = x_prompt.shape
    bs, ts, _ = x_sample.shape
    n_p, n_s = bp * tp, bs * ts
    past_len = page_table.shape[1] * PAGE_SIZE
    row = lambda v: v.reshape(1, -1)

    w_pool_b = w_pool[0].astype(BF16)
    w_ffn_in_b, w_ffn_out_b = w_ffn_in.astype(BF16), w_ffn_out.astype(BF16)
    w_exp_in_b, w_exp_out_b = w_exp_in[0].astype(BF16), w_exp_out[0].astype(BF16)
    w_kv_b = w_kv.astype(BF16)
    w_q_b = w_qg[0][:, :N_HEADS * HEAD_DIM].astype(BF16)
    w_g_b = _gate_weights(w_qg[0]).astype(BF16)
    w_o_b = w_o[0].astype(BF16)
    w1_f, pe_f = _compress_weights(w_cmp1, pe_cmp)
    w1_b, pe_b, w2_b = w1_f.astype(BF16), pe_f.astype(BF16), w_cmp2.astype(BF16)
    w1_parts = jnp.stack(_split_bits(w1_f), axis=1)
    wr = jnp.pad(w_router[0], ((0, 0), (0, LANES - N_EXPERTS)))
    b_r = jnp.pad(b_router[0], (0, LANES - N_EXPERTS)).reshape(1, LANES)
    zero_tiles = lambda n, tm: jnp.zeros((n // tm + 1,), jnp.int32).at[n // tm].set(n // tm)

    h1_p, tail_p = _pool_prompt(x_prompt, row(norm_mix[0]), w_pool_b, row(pool_scale[0]))
    h1_s, pool_s = _pool_sample(x_sample, state_pool[0], row(norm_mix[0]), w_pool[0], row(pool_scale[0]))
    h1_p, h1_s = h1_p.reshape(n_p, d), h1_s.reshape(n_s, d)

    tm_p, tm_s = 512, n_s
    h2_p = _ffn(h1_p, row(norm_ffn[0]), w_ffn_in_b, w_ffn_out_b, zero_tiles(n_p, tm_p), norm=True, residual=True, tm=tm_p)
    h2_s = _ffn(h1_s, row(norm_ffn[0]), w_ffn_in, w_ffn_out, zero_tiles(n_s, tm_s), norm=True, residual=True, tm=tm_s,
                tf=256, precise=True)

    kv_p, q_p, gate_p, kvb_p, vt_p = _proj(h2_p, row(norm_kv), row(norm_mix[1]), w_kv_b, w_q_b, w_g_b, tm_p)
    kv_s, q_s, gate_s = _proj(h2_s, row(norm_kv), row(norm_mix[1]), w_kv, w_qg[0][:, :N_HEADS * HEAD_DIM],
                                 _gate_weights(w_qg[0]), tm_s, precise=True)

    dummy_table = jnp.zeros((1, 1), jnp.int32)
    kv_p3 = kv_p.reshape(bp, tp, KV_WIDTH)
    kvc_p = _compress(
        [kv_p3], [pl.BlockSpec((1, tp, LANES), lambda bi, c, j, tb: (bi, 0, c))], (bp, 4, 1), dummy_table,
        w1_b[:, None], pe_b, w2_b, tp, tp // CMP_STRIDE, paged=False)
    msel_p = _selection_rows(tp // CMP_STRIDE, -(-(tp // SEL_BLOCK) // 8) * 8).T
    attn_p = _attn_prompt(q_p.reshape(bp, tp, d), gate_p.reshape(bp, tp, 2 * LANES), kvc_p,
                          kvb_p.reshape(bp, tp, KV_WIDTH), vt_p.reshape(bp, tp // KEY_CHUNK, 2 * KV_HALF, KEY_CHUNK), msel_p)

    n_pages = page_table.shape[1]
    pages_step = 16
    cache_cmp_t = cache_cmp.transpose(0, 2, 3, 4, 1)
    cache_slc_t = cache_slc.transpose(0, 2, 3, 4, 1)
    cache_win_t = cache_win.transpose(0, 2, 3, 4, 1)
    pages_cmp = 32
    page_spec = lambda pi: pl.BlockSpec(
        (1, 1, 2, HEAD_DIM, PAGE_SIZE), lambda bi, c, j, tb: (tb[bi, j * pages_cmp + pi], c // 2, c % 2, 0, 0))
    kvc_s = _compress([cache_cmp_t] * pages_cmp, [page_spec(pi) for pi in range(pages_cmp)],
                      (bs, 4, n_pages // pages_cmp), page_table, w1_parts, pe_f, w_cmp2, PAGE_SIZE, past_len // CMP_STRIDE,
                      paged=True)
    n_slc_s = past_len // SEL_BLOCK + 1
    sel_width = ((n_slc_s + LANES - 1) // LANES) * LANES
    msel_s = _selection_rows(past_len // CMP_STRIDE, sel_width)
    q_s4 = q_s.reshape(bs, ts, N_KV_HEADS, HEADS_PER_KV, HEAD_DIM).transpose(0, 2, 3, 1, 4)
    q_s4 = jnp.pad(q_s4, ((0, 0), (0, 0), (0, 0), (0, T_PAD - ts), (0, 0)))
    q32 = q_s4.reshape(bs, N_KV_HEADS, HEADS_PER_KV * T_PAD, HEAD_DIM)
    o_cmp_s, sel_s = _attn_sample_cmp(q32, kvc_s, msel_s, past_len, n_slc_s)
    head_mask = jnp.eye(N_KV_HEADS, dtype=F32)[:, None, :, None]
    q_st = (q32[:, :, :, None, :] * head_mask).reshape(bs, S_ROWS, KV_HALF)
    steps = n_pages // pages_step
    sel_rows = jnp.broadcast_to(sel_s[:, :, None, :, :2 * n_pages], (bs, N_KV_HEADS, HEADS_PER_KV, T_PAD, 2 * n_pages))
    sel_steps = sel_rows.reshape(bs, S_ROWS, steps, 2 * pages_step).transpose(0, 2, 1, 3)
    sel_steps = jnp.pad(sel_steps, ((0, 0), (0, 0), (0, 0), (0, LANES - 2 * pages_step)))
    kv_s3 = kv_s.reshape(bs, ts, N_BRANCH, BRANCH_WIDTH)
    kv_new = kv_s3[:, :, 1:3].transpose(0, 2, 1, 3)
    kv_new = jnp.pad(kv_new, ((0, 0), (0, 0), (0, LANES - ts), (0, 0)))
    gate_s5 = gate_s.reshape(bs, ts, 2, LANES)[..., :2 * HEADS_PER_KV * N_BRANCH]
    gate_s5 = gate_s5.reshape(bs, ts, N_KV_HEADS, HEADS_PER_KV, N_BRANCH).transpose(0, 2, 3, 1, 4)
    gate_rows = jnp.pad(gate_s5, ((0, 0), (0, 0), (0, 0), (0, T_PAD - ts), (0, LANES - N_BRANCH))).reshape(bs, S_ROWS, LANES)
    assert cache_win.shape[1] == WINDOW and ts <= T_PAD
    o_s = _attn_sample(page_table, cache_slc_t, q_st, sel_steps, kv_new, ts, cache_win_t,
                       o_cmp_s.reshape(bs, S_ROWS, HEAD_DIM), gate_rows, pages_step)
    attn_s = o_s.reshape(bs, N_KV_HEADS, HEADS_PER_KV, T_PAD, HEAD_DIM)[:, :, :, :ts].transpose(0, 3, 1, 2, 4).reshape(n_s, d)

    h3_p, u4_p, route_p = _oproj_router(h2_p, attn_p.reshape(n_p, d), w_o_b, row(norm_ffn[1]), wr, b_r, tm_p)
    h3_s, u4_s, route_s = _oproj_router(h2_s, attn_s, w_o[0], row(norm_ffn[1]), wr, b_r, tm_s, precise=True)
    tm_e = 512
    dest, tile_expert, n_rows = _dispatch(jnp.concatenate([route_p, route_s], axis=0), tm_e)
    dest_p, dest_s = dest[:2 * n_p], dest[2 * n_p:]
    xs = jnp.zeros((n_rows, d), F32)
    xs = _scatter_rows(u4_p, dest_p, xs, tm_p)
    xs = _scatter_rows(u4_s, dest_s, xs, tm_s)
    y_sorted = _ffn(xs, row(norm_ffn[1]), w_exp_in_b, w_exp_out_b, tile_expert, norm=False, residual=False, tm=tm_e)
    y_p = _combine(h3_p, y_sorted, dest_p, route_p, row(norm_final), tm_p)
    y_s = _combine(h3_s, y_sorted, dest_s, route_s, row(norm_final), tm_s)

    kv_p5 = kv_p.reshape(bp, tp, N_BRANCH, 2, N_KV_HEADS, HEAD_DIM)
    kv_s5 = kv_s.reshape(bs, ts, N_BRANCH, 2, N_KV_HEADS, HEAD_DIM)
    win_buf = cache_win.shape[1]
    win_sample = jnp.concatenate([cache_win, kv_s5[:, :, 2]], axis=1)[:, -win_buf:]
    return (y_p.reshape(bp, tp, d), y_s.reshape(bs, ts, d),
            kv_p5[:, :, 0], kv_s5[:, :, 0], kv_p5[:, :, 1], kv_s5[:, :, 1],
            kv_p5[:, -min(WINDOW, tp):, 2], win_sample,
            tail_p[None, :, 1:], pool_s[None])
```
